```python
import jax
import jax.numpy as jnp
from jax import lax
import numpy as np

D_MODEL = 1024
BATCH = 1
SEQ = 16384
DEPTH = 4

N_MIXERS = 4
N_HEADS = 16
HEAD_DIM = D_MODEL // N_HEADS
D_FF = ((8 * D_MODEL // 3 + 127) // 128) * 128
SHORT_CONV_K = 3
CONFORMER_K = 31
FFN_CONV_K = 3
FOX_Q_BLOCK = 128
MOBA_BLOCK = 256
MOBA_TOPK = 3
MOBA_Q_CHUNK = 64
ALIBI_MAX_EXP = 8.0
RMS_EPS = 1e-6
LN_EPS = 1e-5
NEG_INF = -1e30

kernel_name = 'hybrid_interleaved_conv_fox_moba_block'


def rmsnorm(x, g):
    xf = x.astype(jnp.float32)
    y = xf * lax.rsqrt(jnp.mean(xf * xf, axis=-1, keepdims=True) + RMS_EPS)
    return (y * g.astype(jnp.float32)).astype(x.dtype)


def layernorm(x, g, b):
    xf = x.astype(jnp.float32)
    xc = xf - jnp.mean(xf, axis=-1, keepdims=True)
    y = xc * lax.rsqrt(jnp.mean(xc * xc, axis=-1, keepdims=True) + LN_EPS)
    return (y * g.astype(jnp.float32) + b.astype(jnp.float32)).astype(x.dtype)


def causal_dwconv(x, w):
    k = w.shape[0]
    return lax.conv_general_dilated(
        x, w[:, None, :].astype(x.dtype), window_strides=(1,), padding=[(k - 1, 0)],
        dimension_numbers=('NWC', 'WIO', 'NWC'), feature_group_count=x.shape[-1])


def to_heads(t):
    b, s, _ = t.shape
    return t.reshape(b, s, N_HEADS, HEAD_DIM).transpose(0, 2, 1, 3)


def from_head_blocks(o):
    n, b, h, blk, dh = o.shape
    return o.transpose(1, 0, 3, 2, 4).reshape(b, n * blk, h * dh)


def alibi_slopes():
    return jnp.exp2(-ALIBI_MAX_EXP * jnp.arange(1, N_HEADS + 1, dtype=jnp.float32) / N_HEADS)


def short_conv_mixer(h, w_in, w_conv, w_out):
    b_gate, c_gate, xh = jnp.split(h @ w_in, 3, axis=-1)
    return (b_gate * causal_dwconv(c_gate * xh, w_conv)) @ w_out


def conformer_conv_mixer(h, w_in, w_dw, b_dw, ln_g, ln_b, w_out):
    a, g = jnp.split(h @ w_in, 2, axis=-1)
    u = causal_dwconv(a * jax.nn.sigmoid(g), w_dw) + b_dw
    return jax.nn.silu(layernorm(u, ln_g, ln_b)) @ w_out


def fox_attention(h, w_in, b_f, g_q, g_k, w_out):
    b, s, d = h.shape
    q, k, v, f_logit = jnp.split(h @ w_in, [d, 2 * d, 3 * d], axis=-1)
    q = rmsnorm(to_heads(q), g_q)
    k = rmsnorm(to_heads(k), g_k)
    v = to_heads(v)
    log_f = jax.nn.log_sigmoid((f_logit + b_f).astype(jnp.float32))
    cum_f = jnp.cumsum(log_f, axis=1).transpose(0, 2, 1)
    n_blk = s // FOX_Q_BLOCK
    q_blk = q.reshape(b, N_HEADS, n_blk, FOX_Q_BLOCK, HEAD_DIM).transpose(2, 0, 1, 3, 4)
    f_blk = cum_f.reshape(b, N_HEADS, n_blk, FOX_Q_BLOCK).transpose(2, 0, 1, 3)
    k_pos = jnp.arange(s)
    scale = HEAD_DIM ** -0.5

    def block(args):
        q_i, f_i, i = args
        q_pos = i * FOX_Q_BLOCK + jnp.arange(FOX_Q_BLOCK)
        logits = jnp.einsum('bhqd,bhkd->bhqk', q_i, k, preferred_element_type=jnp.float32) * scale
        logits = jnp.where(k_pos[None, :] <= q_pos[:, None],
                           logits + (f_i[..., :, None] - cum_f[:, :, None, :]), NEG_INF)
        p = jax.nn.softmax(logits, axis=-1)
        return jnp.einsum('bhqk,bhkd->bhqd', p.astype(v.dtype), v)

    o = lax.map(block, (q_blk, f_blk, jnp.arange(n_blk)))
    return from_head_blocks(o) @ w_out


def moba_attention(h, w_in, g_q, g_k, w_out):
    b, s, d = h.shape
    q, k, v = jnp.split(h @ w_in, 3, axis=-1)
    q = rmsnorm(to_heads(q), g_q)
    k = rmsnorm(to_heads(k), g_k)
    v = to_heads(v)
    n_kb = -(-s // MOBA_BLOCK)
    pad = ((0, 0), (0, 0), (0, n_kb * MOBA_BLOCK - s), (0, 0))
    kp = jnp.pad(k, pad)
    vp = jnp.pad(v, pad)
    k_blocks = kp.reshape(b, N_HEADS, n_kb, MOBA_BLOCK, HEAD_DIM)
    k_mean = jnp.mean(k_blocks.astype(jnp.float32), axis=3)
    kb_flat = k_blocks.reshape(b * N_HEADS, n_kb, MOBA_BLOCK, HEAD_DIM)
    vb_flat = vp.reshape(b * N_HEADS, n_kb, MOBA_BLOCK, HEAD_DIM)
    top_k = min(MOBA_TOPK, n_kb)
    slopes = alibi_slopes()[None, :, None, None]
    blk_ids = jnp.arange(n_kb)
    offs = jnp.arange(MOBA_BLOCK)
    scale = HEAD_DIM ** -0.5
    gather = jax.vmap(lambda blocks, ids: blocks[ids])

    def chunk(ci):
        q0 = ci * MOBA_Q_CHUNK
        own = q0 // MOBA_BLOCK
        q_pos = q0 + jnp.arange(MOBA_Q_CHUNK)
        qc = lax.dynamic_slice_in_dim(q, q0, MOBA_Q_CHUNK, axis=2)
        gate = jnp.einsum('bhqd,bhnd->bhqn', qc.astype(jnp.float32), k_mean)
        gate = jnp.where(blk_ids < own, gate, NEG_INF)
        _, sel = lax.top_k(gate, top_k)
        valid = sel < own
        ids = sel.reshape(b * N_HEADS, MOBA_Q_CHUNK * top_k)
        k_sel = gather(kb_flat, ids).reshape(b, N_HEADS, MOBA_Q_CHUNK, top_k, MOBA_BLOCK, HEAD_DIM)
        v_sel = gather(vb_flat, ids).reshape(b, N_HEADS, MOBA_Q_CHUNK, top_k, MOBA_BLOCK, HEAD_DIM)
        pos_sel = sel[..., None] * MOBA_BLOCK + offs
        s_sel = jnp.einsum('bhqd,bhqjkd->bhqjk', qc, k_sel, preferred_element_type=jnp.float32) * scale
        s_sel = s_sel - slopes[..., None] * (q_pos[:, None, None] - pos_sel).astype(jnp.float32)
        s_sel = jnp.where(valid[..., None], s_sel, NEG_INF).reshape(b, N_HEADS, MOBA_Q_CHUNK, top_k * MOBA_BLOCK)
        k_own = lax.dynamic_slice_in_dim(kp, own * MOBA_BLOCK, MOBA_BLOCK, axis=2)
        v_own = lax.dynamic_slice_in_dim(vp, own * MOBA_BLOCK, MOBA_BLOCK, axis=2)
        own_pos = own * MOBA_BLOCK + offs
        s_own = jnp.einsum('bhqd,bhkd->bhqk', qc, k_own, preferred_element_type=jnp.float32) * scale
        s_own = jnp.where(own_pos[None, :] <= q_pos[:, None],
                          s_own - slopes * (q_pos[:, None] - own_pos[None, :]).astype(jnp.float32), NEG_INF)
        p = jax.nn.softmax(jnp.concatenate([s_sel, s_own], axis=-1), axis=-1).astype(v.dtype)
        p_sel = p[..., :top_k * MOBA_BLOCK].reshape(b, N_HEADS, MOBA_Q_CHUNK, top_k, MOBA_BLOCK)
        p_own = p[..., top_k * MOBA_BLOCK:]
        return (jnp.einsum('bhqjk,bhqjkd->bhqd', p_sel, v_sel)
                + jnp.einsum('bhqk,bhkd->bhqd', p_own, v_own))

    o = lax.map(chunk, jnp.arange(s // MOBA_Q_CHUNK))
    return from_head_blocks(o) @ w_out


def conv_ffn(h, w_up, w_conv, w_down):
    u, g = jnp.split(h @ w_up, 2, axis=-1)
    return (jax.nn.silu(causal_dwconv(u, w_conv)) * g) @ w_down


def setup_inputs(seed: int = 0) -> dict:
    key = jax.random.key(seed)
    ks = iter(jax.random.split(key, 40))
    f32 = jnp.float32
    D = D_MODEL

    def w(shape, fan_in, mult=1.0):
        return jax.random.normal(next(ks), shape, f32) * (mult * fan_in ** -0.5)

    def gain(shape):
        return 1.0 + 0.02 * jax.random.normal(next(ks), shape, f32)

    def bias(shape):
        return 0.02 * jax.random.normal(next(ks), shape, f32)

    n_a, n_b, n_c, n_d = [len(range(m, DEPTH, N_MIXERS)) for m in range(N_MIXERS)]
    return {
        'x': jax.random.normal(next(ks), (BATCH, SEQ, D), f32),
        'c': jax.random.normal(next(ks), (BATCH, D), f32),
        'ln_mix_g': gain((DEPTH, D)),
        'ln_ffn_g': gain((DEPTH, D)),
        'ada_w': w((DEPTH, D, 6 * D), D, 0.5),
        'ada_b': bias((DEPTH, 6 * D)),
        'ffn_up': w((DEPTH, D, 2 * D_FF), D),
        'ffn_conv': w((DEPTH, FFN_CONV_K, D_FF), FFN_CONV_K),
        'ffn_down': w((DEPTH, D_FF, D), D_FF),
        'sc_in': w((n_a, D, 3 * D), D),
        'sc_conv': w((n_a, SHORT_CONV_K, D), SHORT_CONV_K),
        'sc_out': w((n_a, D, D), D),
        'cf_in': w((n_b, D, 2 * D), D),
        'cf_dw': w((n_b, CONFORMER_K, D), CONFORMER_K),
        'cf_dw_b': bias((n_b, D)),
        'cf_ln_g': gain((n_b, D)),
        'cf_ln_b': bias((n_b, D)),
        'cf_out': w((n_b, D, D), D),
        'fox_in': w((n_c, D, 3 * D + N_HEADS), D),
        'fox_bf': jax.random.uniform(next(ks), (n_c, N_HEADS), f32, 1.0, 4.0),
        'fox_qn': gain((n_c, HEAD_DIM)),
        'fox_kn': gain((n_c, HEAD_DIM)),
        'fox_out': w((n_c, D, D), D),
        'moba_in': w((n_d, D, 3 * D), D),
        'moba_qn': gain((n_d, HEAD_DIM)),
        'moba_kn': gain((n_d, HEAD_DIM)),
        'moba_out': w((n_d, D, D), D),
    }


def reference(x, c, ln_mix_g, ln_ffn_g, ada_w, ada_b, ffn_up, ffn_conv, ffn_down,
              sc_in, sc_conv, sc_out, cf_in, cf_dw, cf_dw_b, cf_ln_g, cf_ln_b, cf_out,
              fox_in, fox_bf, fox_qn, fox_kn, fox_out, moba_in, moba_qn, moba_kn, moba_out):
    cond = jax.nn.silu(c)
    for i in range(DEPTH):
        kind = i % N_MIXERS
        j = i // N_MIXERS
        mod = cond @ ada_w[i] + ada_b[i]
        sh1, sc1, gt1, sh2, sc2, gt2 = [m[:, None, :] for m in jnp.split(mod, 6, axis=-1)]
        h = rmsnorm(x, ln_mix_g[i]) * (1.0 + sc1) + sh1
        if kind == 0:
            y = short_conv_mixer(h, sc_in[j], sc_conv[j], sc_out[j])
        elif kind == 1:
            y = conformer_conv_mixer(h, cf_in[j], cf_dw[j], cf_dw_b[j], cf_ln_g[j], cf_ln_b[j], cf_out[j])
        elif kind == 2:
            y = fox_attention(h, fox_in[j], fox_bf[j], fox_qn[j], fox_kn[j], fox_out[j])
        else:
            y = moba_attention(h, moba_in[j], moba_qn[j], moba_kn[j], moba_out[j])
        x = x + gt1 * y
        h = rmsnorm(x, ln_ffn_g[i]) * (1.0 + sc2) + sh2
        x = x + gt2 * conv_ffn(h, ffn_up[i], ffn_conv[i], ffn_down[i])
    return x
```

```python
import functools

import jax
import jax.numpy as jnp
from jax import lax
from jax.experimental import pallas as pl
from jax.experimental.pallas import tpu as pltpu

D = 1024
N_HEADS = 16
HEAD_DIM = D // N_HEADS
N_PAIRS = N_HEADS // 2
D_FF = ((8 * D // 3 + 127) // 128) * 128
CONFORMER_K = 31
MOBA_BLOCK = 256
MOBA_TOPK = 3
ALIBI_MAX_EXP = 8.0
RMS_EPS = 1e-6
LN_EPS = 1e-5
NEG_INF = -1e30
SCALE = HEAD_DIM ** -0.5

LANES = 128
SUBLANES = 8
MXU_W = 256
VMEM_LIMIT = 56 * 1024 * 1024

F32 = jnp.float32
BF16 = jnp.bfloat16

_dot = functools.partial(jnp.dot, preferred_element_type=F32)


def _dot_nt(a, b):
    return lax.dot_general(a, b, (((1,), (1,)), ((), ())), preferred_element_type=F32)


def _resident(shape):
    nd = len(shape)
    return pl.BlockSpec(shape, lambda *_: (0,) * nd, pipeline_mode=pl.Buffered(1))


def _rows(tm, width):
    return pl.BlockSpec((tm, width), lambda i: (i, 0))


def _params(*sem):
    return pltpu.CompilerParams(dimension_semantics=sem, vmem_limit_bytes=VMEM_LIMIT)


def _rms_mod(x, g, scale, shift):
    ms = jnp.mean(x * x, axis=-1, keepdims=True)
    return (x * lax.rsqrt(ms + RMS_EPS) * g) * (1.0 + scale) + shift


def _shifted(ext, s, halo, tm):
    a, b = divmod(s, SUBLANES)
    r = pltpu.roll(ext, b, axis=0) if b else ext
    lo = halo - a * SUBLANES
    return r[lo:lo + tm, :]


def _split2(z):
    hi = z.astype(BF16)
    lo = (z - hi.astype(F32)).astype(BF16)
    return hi, lo


def _split3(z):
    hi = z.astype(BF16)
    r = z - hi.astype(F32)
    mid = r.astype(BF16)
    lo = (r - mid.astype(F32)).astype(BF16)
    return hi, mid, lo


def _ada_kernel(c_ref, w_ref, b_ref, o_ref):
    cond = jax.nn.silu(c_ref[...])
    o_ref[0, 0] = jnp.sum(w_ref[0] * cond, axis=0, keepdims=True) + b_ref[0, 0]


def _ada(c, ada_w, ada_b):
    depth = ada_w.shape[0]
    out = pl.pallas_call(
        _ada_kernel,
        grid=(depth, 6),
        in_specs=[
            pl.BlockSpec((D, 1), lambda l, j: (0, 0)),
            pl.BlockSpec((1, D, D), lambda l, j: (l, 0, j)),
            pl.BlockSpec((1, 1, 1, D), lambda l, j: (l, j, 0, 0)),
        ],
        out_specs=pl.BlockSpec((1, 1, 1, D), lambda l, j: (l, j, 0, 0)),
        out_shape=jax.ShapeDtypeStruct((depth, 6, 1, D), F32),
        compiler_params=_params("parallel", "parallel"),
        name="ada",
    )(c.reshape(D, 1), ada_w, ada_b.reshape(depth, 6, 1, D))
    return out.reshape(depth, 6, D)


def _ffn_kernel(*refs, tm, cw, pre_proj):
    if pre_proj:
        x_ref, a_ref, wo_ref, mod_ref, g_ref, wup_ref, wc_ref, wdn_ref, o_ref, carry_ref = refs
    else:
        x_ref, mod_ref, g_ref, wup_ref, wc_ref, wdn_ref, o_ref, carry_ref = refs

    @pl.when(pl.program_id(0) == 0)
    def _():
        carry_ref[...] = jnp.zeros_like(carry_ref)

    x = x_ref[...]
    if pre_proj:
        x = x + mod_ref[2:3, :] * _dot(a_ref[...], wo_ref[...])
    h = _rms_mod(x, g_ref[...], mod_ref[4:5, :], mod_ref[3:4, :]).astype(BF16)
    acc = jnp.zeros((tm, D), F32)
    for c in range(D_FF // cw):
        lo = c * cw
        u = _dot(h, wup_ref[:, lo:lo + cw])
        gate = _dot(h, wup_ref[:, D_FF + lo:D_FF + lo + cw])
        ext = jnp.concatenate([carry_ref[:, lo:lo + cw], u], axis=0)
        carry_ref[:, lo:lo + cw] = u[tm - SUBLANES:, :]
        conv = (wc_ref[0:1, lo:lo + cw] * _shifted(ext, 2, SUBLANES, tm)
                + wc_ref[1:2, lo:lo + cw] * _shifted(ext, 1, SUBLANES, tm)
                + wc_ref[2:3, lo:lo + cw] * u)
        act = (conv * jax.nn.sigmoid(conv)) * gate
        acc = acc + _dot(act.astype(BF16), wdn_ref[lo:lo + cw, :])
    o_ref[...] = x + mod_ref[5:6, :] * acc


def _ffn(x, mod, g, w_up, w_conv, w_down, pre=None, *, tm=512, cw=256):
    s = x.shape[0]
    ins, specs = [x], [_rows(tm, D)]
    if pre is not None:
        a, w_o = pre
        ins += [a, w_o]
        specs += [_rows(tm, D), _resident((D, D))]
    ins += [mod, g.reshape(1, D), w_up, w_conv, w_down]
    specs += [_resident((6, D)), _resident((1, D)), _resident((D, 2 * D_FF)),
              _resident((3, D_FF)), _resident((D_FF, D))]
    return pl.pallas_call(
        functools.partial(_ffn_kernel, tm=tm, cw=cw, pre_proj=pre is not None),
        grid=(s // tm,),
        in_specs=specs,
        out_specs=_rows(tm, D),
        out_shape=jax.ShapeDtypeStruct((s, D), F32),
        scratch_shapes=[pltpu.VMEM((SUBLANES, D_FF), F32)],
        compiler_params=_params("arbitrary"),
        name="ffn_proj" if pre is not None else "ffn",
    )(*ins)


def _sc_kernel(x_ref, mod_ref, g_ref, win_ref, wc_ref, wout_ref, o_ref, carry_ref, *, tm, cw):
    @pl.when(pl.program_id(0) == 0)
    def _():
        carry_ref[...] = jnp.zeros_like(carry_ref)

    x = x_ref[...]
    h = _rms_mod(x, g_ref[...], mod_ref[1:2, :], mod_ref[0:1, :]).astype(BF16)
    acc = jnp.zeros((tm, D), F32)
    for c in range(D // cw):
        lo = c * cw
        b_gate = _dot(h, win_ref[:, lo:lo + cw])
        c_gate = _dot(h, win_ref[:, D + lo:D + lo + cw])
        xh = _dot(h, win_ref[:, 2 * D + lo:2 * D + lo + cw])
        u = c_gate * xh
        ext = jnp.concatenate([carry_ref[:, lo:lo + cw], u], axis=0)
        carry_ref[:, lo:lo + cw] = u[tm - SUBLANES:, :]
        conv = (wc_ref[0:1, lo:lo + cw] * _shifted(ext, 2, SUBLANES, tm)
                + wc_ref[1:2, lo:lo + cw] * _shifted(ext, 1, SUBLANES, tm)
                + wc_ref[2:3, lo:lo + cw] * u)
        acc = acc + _dot((b_gate * conv).astype(BF16), wout_ref[lo:lo + cw, :])
    o_ref[...] = x + mod_ref[2:3, :] * acc


def _sc_mixer(x, mod, g, w_in, w_conv, w_out, *, tm=512, cw=256):
    s = x.shape[0]
    return pl.pallas_call(
        functools.partial(_sc_kernel, tm=tm, cw=cw),
        grid=(s // tm,),
        in_specs=[_rows(tm, D), _resident((6, D)), _resident((1, D)), _resident((D, 3 * D)),
                  _resident((3, D)), _resident((D, D))],
        out_specs=_rows(tm, D),
        out_shape=jax.ShapeDtypeStruct((s, D), F32),
        scratch_shapes=[pltpu.VMEM((SUBLANES, D), F32)],
        compiler_params=_params("arbitrary"),
        name="sc_mixer",
    )(x, mod, g.reshape(1, D), w_in, w_conv, w_out)


CF_HALO = 32


def _cf_kernel(x_ref, mod_ref, g_ref, win_ref, wdw_ref, bdw_ref, lng_ref, lnb_ref, wout_ref,
               o_ref, carry_ref, u_ref, *, tm, cw):
    @pl.when(pl.program_id(0) == 0)
    def _():
        carry_ref[...] = jnp.zeros_like(carry_ref)

    x = x_ref[...]
    h = _rms_mod(x, g_ref[...], mod_ref[1:2, :], mod_ref[0:1, :]).astype(BF16)
    for c in range(D // cw):
        lo = c * cw
        a = _dot(h, win_ref[:, lo:lo + cw])
        gate = _dot(h, win_ref[:, D + lo:D + lo + cw])
        glu = a * jax.nn.sigmoid(gate)
        ext = jnp.concatenate([carry_ref[:, lo:lo + cw], glu], axis=0)
        carry_ref[:, lo:lo + cw] = glu[tm - CF_HALO:, :]
        conv = jnp.zeros((tm, cw), F32) + bdw_ref[:, lo:lo + cw]
        for b in range(SUBLANES):
            r = pltpu.roll(ext, b, axis=0) if b else ext
            for a8 in range(CF_HALO // SUBLANES):
                s = a8 * SUBLANES + b
                if s < CONFORMER_K:
                    k = CONFORMER_K - 1 - s
                    start = CF_HALO - a8 * SUBLANES
                    conv = conv + wdw_ref[k:k + 1, lo:lo + cw] * r[start:start + tm, :]
        u_ref[:, lo:lo + cw] = conv
    u = u_ref[...]
    uc = u - jnp.mean(u, axis=-1, keepdims=True)
    y = uc * lax.rsqrt(jnp.mean(uc * uc, axis=-1, keepdims=True) + LN_EPS)
    y = y * lng_ref[...] + lnb_ref[...]
    y = (y * jax.nn.sigmoid(y)).astype(BF16)
    o_ref[...] = x + mod_ref[2:3, :] * _dot(y, wout_ref[...])


def _cf_mixer(x, mod, g, w_in, w_dw, b_dw, ln_g, ln_b, w_out, *, tm=512, cw=256):
    s = x.shape[0]
    return pl.pallas_call(
        functools.partial(_cf_kernel, tm=tm, cw=cw),
        grid=(s // tm,),
        in_specs=[_rows(tm, D), _resident((6, D)), _resident((1, D)), _resident((D, 2 * D)),
                  _resident((CONFORMER_K, D)), _resident((1, D)), _resident((1, D)),
                  _resident((1, D)), _resident((D, D))],
        out_specs=_rows(tm, D),
        out_shape=jax.ShapeDtypeStruct((s, D), F32),
        scratch_shapes=[pltpu.VMEM((CF_HALO, D), F32), pltpu.VMEM((tm, D), F32)],
        compiler_params=_params("arbitrary"),
        name="cf_mixer",
    )(x, mod, g.reshape(1, D), w_in, w_dw, b_dw.reshape(1, D), ln_g.reshape(1, D),
      ln_b.reshape(1, D), w_out)


def _group_mean_matrix(n):
    r = lax.broadcasted_iota(jnp.int32, (n, n), 0) // HEAD_DIM
    c = lax.broadcasted_iota(jnp.int32, (n, n), 1) // HEAD_DIM
    return jnp.where(r == c, 1.0 / HEAD_DIM, 0.0).astype(BF16)


def _head_rms(z, gm):
    hi, lo = _split2(z * z)
    ms = _dot(hi, gm) + _dot(lo, gm)
    return z * lax.rsqrt(ms + RMS_EPS)


def _fox_proj_kernel(x_ref, mod_ref, g_ref, w_ref, bf_ref, qn_ref, kn_ref,
                     q_ref, k_ref, v_ref, f_ref, carry_ref, tri_ref, *, tm, cw):
    @pl.when(pl.program_id(0) == 0)
    def _():
        carry_ref[...] = jnp.zeros_like(carry_ref)
        r = lax.broadcasted_iota(jnp.int32, (tm, tm), 0)
        c = lax.broadcasted_iota(jnp.int32, (tm, tm), 1)
        tri_ref[...] = jnp.where(r >= c, 1.0, 0.0).astype(BF16)

    h = _rms_mod(x_ref[...], g_ref[...], mod_ref[1:2, :], mod_ref[0:1, :]).astype(BF16)
    gm = _group_mean_matrix(cw)
    for c in range(D // cw):
        lo = c * cw
        q = _head_rms(_dot(h, w_ref[:, lo:lo + cw]), gm)
        q_ref[:, lo:lo + cw] = (q * qn_ref[:, lo:lo + cw] * SCALE).astype(BF16)
        k = _head_rms(_dot(h, w_ref[:, D + lo:D + lo + cw]), gm)
        k_ref[:, lo:lo + cw] = (k * kn_ref[:, lo:lo + cw]).astype(BF16)
        v_ref[:, lo:lo + cw] = _dot(h, w_ref[:, 2 * D + lo:2 * D + lo + cw]).astype(BF16)
    fl = _dot(h, w_ref[:, 3 * D:3 * D + LANES]) + bf_ref[...]
    lf = jnp.minimum(fl, 0.0) - jnp.log1p(jnp.exp(-jnp.abs(fl)))
    tri = tri_ref[...]
    hi, mid, lo3 = _split3(lf)
    cum = _dot(tri, hi) + _dot(tri, mid) + _dot(tri, lo3) + carry_ref[0:1, :]
    f_ref[...] = cum
    carry_ref[...] = jnp.broadcast_to(cum[tm - 1:tm, :], carry_ref.shape)


def _fox_proj(x, mod, g, w_in, b_f, qn, kn, *, tm=512, cw=256):
    s = x.shape[0]
    w = jnp.pad(w_in, ((0, 0), (0, LANES - N_HEADS))).astype(BF16)
    bf = jnp.pad(b_f, (0, LANES - N_HEADS)).reshape(1, LANES)
    return pl.pallas_call(
        functools.partial(_fox_proj_kernel, tm=tm, cw=cw),
        grid=(s // tm,),
        in_specs=[_rows(tm, D), _resident((6, D)), _resident((1, D)),
                  _resident((D, 3 * D + LANES)), _resident((1, LANES)),
                  _resident((1, D)), _resident((1, D))],
        out_specs=[_rows(tm, D), _rows(tm, D), _rows(tm, D), _rows(tm, LANES)],
        out_shape=[jax.ShapeDtypeStruct((s, D), BF16)] * 3 + [jax.ShapeDtypeStruct((s, LANES), F32)],
        scratch_shapes=[pltpu.VMEM((SUBLANES, LANES), F32), pltpu.VMEM((tm, tm), BF16)],
        compiler_params=_params("arbitrary"),
        name="fox_proj",
    )(x, mod, g.reshape(1, D), w, bf, jnp.tile(qn, N_HEADS).reshape(1, D),
      jnp.tile(kn, N_HEADS).reshape(1, D))


def _moba_proj_kernel(x_ref, mod_ref, g_ref, w_ref, qn_ref, kn_ref,
                      q_ref, k_ref, v_ref, qf_ref, km_ref, *, tm, cw):
    h = _rms_mod(x_ref[...], g_ref[...], mod_ref[1:2, :], mod_ref[0:1, :]).astype(BF16)
    gm = _group_mean_matrix(cw)
    for c in range(D // cw):
        lo = c * cw
        q = _head_rms(_dot(h, w_ref[:, lo:lo + cw]), gm) * qn_ref[:, lo:lo + cw]
        qf_ref[:, lo:lo + cw] = q
        q_ref[:, lo:lo + cw] = (q * SCALE).astype(BF16)
        k = _head_rms(_dot(h, w_ref[:, D + lo:D + lo + cw]), gm) * kn_ref[:, lo:lo + cw]
        k_ref[:, lo:lo + cw] = k.astype(BF16)
        for b in range(tm // MOBA_BLOCK):
            km_ref[b, :, lo:lo + cw] = jnp.mean(
                k[b * MOBA_BLOCK:(b + 1) * MOBA_BLOCK, :], axis=0, keepdims=True)
        v_ref[:, lo:lo + cw] = _dot(h, w_ref[:, 2 * D + lo:2 * D + lo + cw]).astype(BF16)


def _moba_proj(x, mod, g, w_in, qn, kn, *, tm=512, cw=256):
    s = x.shape[0]
    nb = tm // MOBA_BLOCK
    return pl.pallas_call(
        functools.partial(_moba_proj_kernel, tm=tm, cw=cw),
        grid=(s // tm,),
        in_specs=[_rows(tm, D), _resident((6, D)), _resident((1, D)), _resident((D, 3 * D)),
                  _resident((1, D)), _resident((1, D))],
        out_specs=[_rows(tm, D), _rows(tm, D), _rows(tm, D), _rows(tm, D),
                   pl.BlockSpec((nb, 1, D), lambda i: (i, 0, 0))],
        out_shape=[jax.ShapeDtypeStruct((s, D), BF16)] * 3
        + [jax.ShapeDtypeStruct((s, D), F32), jax.ShapeDtypeStruct((s // MOBA_BLOCK, 1, D), F32)],
        compiler_params=_params("parallel"),
        name="moba_proj",
    )(x, mod, g.reshape(1, D), w_in, jnp.tile(qn, N_HEADS).reshape(1, D),
      jnp.tile(kn, N_HEADS).reshape(1, D))


def _moba_gate_kernel(q_ref, km_ref, sel_ref):
    own = pl.program_id(0)
    t = q_ref.shape[0]
    n_kb = HEAD_DIM
    lane = lax.broadcasted_iota(jnp.int32, (t, LANES), 1)
    blk = lane & (HEAD_DIM - 1)
    zeros_km = jnp.zeros((n_kb, MXU_W), BF16)
    for g4 in range(D // MXU_W):
        lo = g4 * MXU_W
        q = q_ref[:, lo:lo + MXU_W]
        q_lane_head = lax.shift_right_logical(
            lax.broadcasted_iota(jnp.int32, (t, MXU_W), 1), HEAD_DIM.bit_length() - 1)
        km_hi, km_lo = _split2(km_ref[:, lo:lo + MXU_W])
        for pr in range(2):
            gate = jnp.zeros((t, LANES), F32)
            for hh in range(2):
                qh = jnp.where(q_lane_head == 2 * pr + hh, q, 0.0)
                q_hi, q_lo = _split2(qh)
                parts = [zeros_km, zeros_km]
                parts[hh] = km_hi
                rhs_hi = jnp.concatenate(parts, axis=0)
                parts[hh] = km_lo
                rhs_lo = jnp.concatenate(parts, axis=0)
                gate = gate + _dot_nt(q_hi, rhs_hi) + _dot_nt(q_hi, rhs_lo) + _dot_nt(q_lo, rhs_hi)
            keep = blk == own
            for hh in range(2):
                mine = (lane < HEAD_DIM) if hh == 0 else (lane >= HEAD_DIM)
                gh = jnp.where(mine, jnp.where(blk < own, gate, NEG_INF), -jnp.inf)
                for _ in range(MOBA_TOPK):
                    mx = jnp.max(gh, axis=1, keepdims=True)
                    idx = jnp.min(jnp.where(gh == mx, blk, n_kb), axis=1, keepdims=True)
                    pick = mine & (blk == idx)
                    keep = keep | (pick & (idx < own))
                    gh = jnp.where(pick, -jnp.inf, gh)
            col = (2 * g4 + pr) * LANES
            sel_ref[:, col:col + LANES] = jnp.where(keep, 0.0, NEG_INF).astype(BF16)


def _moba_gate(qf, kmean):
    s = qf.shape[0]
    n_kb = s // MOBA_BLOCK
    assert n_kb <= HEAD_DIM, "selection bias layout has HEAD_DIM key-block lanes per head"
    km = jnp.pad(kmean.reshape(n_kb, D), ((0, HEAD_DIM - n_kb), (0, 0)))
    return pl.pallas_call(
        _moba_gate_kernel,
        grid=(n_kb,),
        in_specs=[_rows(MOBA_BLOCK, D), _resident((HEAD_DIM, D))],
        out_specs=_rows(MOBA_BLOCK, D),
        out_shape=jax.ShapeDtypeStruct((s, D), BF16),
        compiler_params=_params("parallel"),
        name="moba_gate",
    )(qf, km)


def _attn_kernel(*refs, t, moba):
    if moba:
        q_ref, k_ref, v_ref, sel_ref, o_ref = refs
    else:
        q_ref, k_ref, v_ref, f_ref, o_ref = refs
    pair = pl.program_id(0)
    i = pl.program_id(1)
    q = q_ref[...]
    lane = lax.broadcasted_iota(jnp.int32, (t, LANES), 1)
    row = lax.broadcasted_iota(jnp.int32, (t, t), 0)
    col = lax.broadcasted_iota(jnp.int32, (t, t), 1)
    kcol = lax.broadcasted_iota(jnp.int32, (1, t), 1)
    outs = []
    for hh in range(2):
        mine = (lane < HEAD_DIM) if hh == 0 else (lane >= HEAD_DIM)
        qh = jnp.where(mine, q, jnp.zeros_like(q))
        if moba:
            qh = jnp.concatenate([qh, sel_ref[...]], axis=1)
            head = (kcol * 0 + (2 * pair + hh + 1)).astype(F32)
            slope = jnp.exp2(-ALIBI_MAX_EXP * head / N_HEADS)
        else:
            f_q0 = f_ref[0, hh:hh + 1, pl.ds(pl.multiple_of(i * t, t), t)][:, 0:1]

        def scores(kt):
            start = pl.multiple_of(kt * t, t)
            kk = k_ref[pl.ds(start, t), :]
            if moba:
                onehot = jnp.where(lane == hh * HEAD_DIM + kt, 1.0, 0.0).astype(BF16)
                kk = jnp.concatenate([kk, onehot], axis=1)
                bias = -(slope * ((i - kt) * t - kcol).astype(F32))
            else:
                bias = f_q0 - f_ref[0, hh:hh + 1, pl.ds(start, t)]
            return _dot_nt(qh, kk) + bias, v_ref[pl.ds(start, t), :]

        s, vv = scores(i)
        s = jnp.where(col <= row, s, NEG_INF)
        m = jnp.max(s, axis=1, keepdims=True)
        p = jnp.exp(s - m)
        l = jnp.sum(p, axis=1, keepdims=True)
        acc = _dot(p.astype(BF16), vv)

        def body(jj, carry):
            m, l, acc = carry
            s, vv = scores(i - 1 - jj)
            m_new = jnp.maximum(m, jnp.max(s, axis=1, keepdims=True))
            alpha = jnp.exp(m - m_new)
            p = jnp.exp(s - m_new)
            l = alpha * l + jnp.sum(p, axis=1, keepdims=True)
            acc = alpha * acc + _dot(p.astype(BF16), vv)
            return m_new, l, acc

        m, l, acc = lax.fori_loop(0, i, body, (m, l, acc))
        outs.append(acc * (1.0 / l))
    o_ref[...] = jnp.where(lane < HEAD_DIM, outs[0], outs[1]).astype(BF16)


def _attention(q, k, v, extra, *, t, moba):
    s = q.shape[0]
    col_block = pl.BlockSpec((s, LANES), lambda p, i: (0, p))
    tile = pl.BlockSpec((t, LANES), lambda p, i: (i, p))
    if moba:
        extra_spec = tile
    else:
        extra_spec = pl.BlockSpec((1, 2, s), lambda p, i: (p, 0, 0))
    return pl.pallas_call(
        functools.partial(_attn_kernel, t=t, moba=moba),
        grid=(N_PAIRS, s // t),
        in_specs=[tile, col_block, col_block, extra_spec],
        out_specs=tile,
        out_shape=jax.ShapeDtypeStruct((s, D), BF16),
        compiler_params=_params("parallel", "parallel"),
        name="moba_attn" if moba else "fox_attn",
    )(q, k, v, extra)


def kernel(x, c, ln_mix_g, ln_ffn_g, ada_w, ada_b, ffn_up, ffn_conv, ffn_down, sc_in, sc_conv, sc_out, cf_in, cf_dw, cf_dw_b, cf_ln_g, cf_ln_b, cf_out, fox_in, fox_bf, fox_qn, fox_kn, fox_out, moba_in, moba_qn, moba_kn, moba_out):
    batch, s, _ = x.shape
    assert batch == 1 and c.shape[0] == 1
    depth = ada_w.shape[0]
    mods = _ada(c, ada_w, ada_b)
    xs = x[0]
    for i in range(depth):
        kind, j = i % 4, i // 4
        mod = mods[i]
        pre = None
        if kind == 0:
            xs = _sc_mixer(xs, mod, ln_mix_g[i], sc_in[j].astype(BF16), sc_conv[j],
                           sc_out[j].astype(BF16))
        elif kind == 1:
            xs = _cf_mixer(xs, mod, ln_mix_g[i], cf_in[j].astype(BF16), cf_dw[j], cf_dw_b[j],
                           cf_ln_g[j], cf_ln_b[j], cf_out[j].astype(BF16))
        elif kind == 2:
            q, k, v, cum_f = _fox_proj(xs, mod, ln_mix_g[i], fox_in[j], fox_bf[j],
                                       fox_qn[j], fox_kn[j])
            f_rows = cum_f[:, :N_HEADS].T.reshape(N_PAIRS, 2, s)
            pre = (_attention(q, k, v, f_rows, t=256, moba=False), fox_out[j].astype(BF16))
        else:
            q, k, v, qf, kmean = _moba_proj(xs, mod, ln_mix_g[i], moba_in[j].astype(BF16),
                                            moba_qn[j], moba_kn[j])
            sel = _moba_gate(qf, kmean)
            pre = (_attention(q, k, v, sel, t=MOBA_BLOCK, moba=True), moba_out[j].astype(BF16))
        xs = _ffn(xs, mod, ln_ffn_g[i], ffn_up[i].astype(BF16), ffn_conv[i],
                  ffn_down[i].astype(BF16), pre)
    return xs[None]
```

```python
import functools

import jax
import jax.numpy as jnp
from jax import lax
from jax.experimental import pallas as pl
from jax.experimental.pallas import tpu as pltpu

D = 1024
N_HEADS = 16
HEAD_DIM = D // N_HEADS
N_PAIRS = N_HEADS // 2
D_FF = ((8 * D // 3 + 127) // 128) * 128
CONFORMER_K = 31
MOBA_BLOCK = 256
MOBA_TOPK = 3
ALIBI_MAX_EXP = 8.0
RMS_EPS = 1e-6
LN_EPS = 1e-5
NEG_INF = -1e30
SCALE = HEAD_DIM ** -0.5
LOG2E = 1.4426950408889634
SKIP_THRESHOLD = 106.0
ATTN_TILE = 512

LANES = 128
SUBLANES = 8
MXU_W = 256
VMEM_LIMIT = 56 * 1024 * 1024

F32 = jnp.float32
BF16 = jnp.bfloat16

_dot = functools.partial(jnp.dot, preferred_element_type=F32)


def _dot_nt(a, b):
    return lax.dot_general(a, b, (((1,), (1,)), ((), ())), preferred_element_type=F32)


def _resident(shape):
    nd = len(shape)
    return pl.BlockSpec(shape, lambda *_: (0,) * nd, pipeline_mode=pl.Buffered(1))


def _rows(tm, width):
    return pl.BlockSpec((tm, width), lambda i: (i, 0))


def _params(*sem):
    return pltpu.CompilerParams(dimension_semantics=sem, vmem_limit_bytes=VMEM_LIMIT)


def _rms_mod(x, g, scale, shift):
    ms = jnp.mean(x * x, axis=-1, keepdims=True)
    return (x * lax.rsqrt(ms + RMS_EPS) * g) * (1.0 + scale) + shift


def _shifted(ext, s, halo, tm):
    a, b = divmod(s, SUBLANES)
    r = pltpu.roll(ext, b, axis=0) if b else ext
    lo = halo - a * SUBLANES
    return r[lo:lo + tm, :]


def _split2(z):
    hi = z.astype(BF16)
    lo = (z - hi.astype(F32)).astype(BF16)
    return hi, lo


def _split3(z):
    hi = z.astype(BF16)
    r = z - hi.astype(F32)
    mid = r.astype(BF16)
    lo = (r - mid.astype(F32)).astype(BF16)
    return hi, mid, lo


def _ada_kernel(c_ref, w_ref, b_ref, o_ref):
    cond = jax.nn.silu(c_ref[...])
    o_ref[0, 0] = jnp.sum(w_ref[0] * cond, axis=0, keepdims=True) + b_ref[0, 0]


def _ada(c, ada_w, ada_b):
    depth = ada_w.shape[0]
    out = pl.pallas_call(
        _ada_kernel,
        grid=(depth, 6),
        in_specs=[
            pl.BlockSpec((D, 1), lambda l, j: (0, 0)),
            pl.BlockSpec((1, D, D), lambda l, j: (l, 0, j)),
            pl.BlockSpec((1, 1, 1, D), lambda l, j: (l, j, 0, 0)),
        ],
        out_specs=pl.BlockSpec((1, 1, 1, D), lambda l, j: (l, j, 0, 0)),
        out_shape=jax.ShapeDtypeStruct((depth, 6, 1, D), F32),
        compiler_params=_params("parallel", "parallel"),
        name="ada",
    )(c.reshape(D, 1), ada_w, ada_b.reshape(depth, 6, 1, D))
    return out.reshape(depth, 6, D)


def _ffn_kernel(*refs, tm, cw, pre_proj):
    if pre_proj:
        x_ref, a_ref, wo_ref, mod_ref, g_ref, wup_ref, wc_ref, wdn_ref, o_ref, carry_ref = refs
    else:
        x_ref, mod_ref, g_ref, wup_ref, wc_ref, wdn_ref, o_ref, carry_ref = refs

    @pl.when(pl.program_id(0) == 0)
    def _():
        carry_ref[...] = jnp.zeros_like(carry_ref)

    x = x_ref[...]
    if pre_proj:
        x = x + mod_ref[2:3, :] * _dot(a_ref[...], wo_ref[...])
    h = _rms_mod(x, g_ref[...], mod_ref[4:5, :], mod_ref[3:4, :]).astype(BF16)
    acc = jnp.zeros((tm, D), F32)
    for c in range(D_FF // cw):
        lo = c * cw
        u = _dot(h, wup_ref[:, lo:lo + cw])
        gate = _dot(h, wup_ref[:, D_FF + lo:D_FF + lo + cw])
        ext = jnp.concatenate([carry_ref[:, lo:lo + cw], u], axis=0)
        carry_ref[:, lo:lo + cw] = u[tm - SUBLANES:, :]
        conv = (wc_ref[0:1, lo:lo + cw] * _shifted(ext, 2, SUBLANES, tm)
                + wc_ref[1:2, lo:lo + cw] * _shifted(ext, 1, SUBLANES, tm)
                + wc_ref[2:3, lo:lo + cw] * u)
        act = (conv * jax.nn.sigmoid(conv)) * gate
        acc = acc + _dot(act.astype(BF16), wdn_ref[lo:lo + cw, :])
    o_ref[...] = x + mod_ref[5:6, :] * acc


def _ffn(x, mod, g, w_up, w_conv, w_down, pre=None, *, tm=512, cw=256):
    s = x.shape[0]
    ins, specs = [x], [_rows(tm, D)]
    if pre is not None:
        a, w_o = pre
        ins += [a, w_o]
        specs += [_rows(tm, D), _resident((D, D))]
    ins += [mod, g.reshape(1, D), w_up, w_conv, w_down]
    specs += [_resident((6, D)), _resident((1, D)), _resident((D, 2 * D_FF)),
              _resident((3, D_FF)), _resident((D_FF, D))]
    return pl.pallas_call(
        functools.partial(_ffn_kernel, tm=tm, cw=cw, pre_proj=pre is not None),
        grid=(s // tm,),
        in_specs=specs,
        out_specs=_rows(tm, D),
        out_shape=jax.ShapeDtypeStruct((s, D), F32),
        scratch_shapes=[pltpu.VMEM((SUBLANES, D_FF), F32)],
        compiler_params=_params("arbitrary"),
        name="ffn_proj" if pre is not None else "ffn",
    )(*ins)


def _sc_kernel(x_ref, mod_ref, g_ref, win_ref, wc_ref, wout_ref, o_ref, carry_ref, *, tm, cw):
    @pl.when(pl.program_id(0) == 0)
    def _():
        carry_ref[...] = jnp.zeros_like(carry_ref)

    x = x_ref[...]
    h = _rms_mod(x, g_ref[...], mod_ref[1:2, :], mod_ref[0:1, :]).astype(BF16)
    acc = jnp.zeros((tm, D), F32)
    for c in range(D // cw):
        lo = c * cw
        b_gate = _dot(h, win_ref[:, lo:lo + cw])
        c_gate = _dot(h, win_ref[:, D + lo:D + lo + cw])
        xh = _dot(h, win_ref[:, 2 * D + lo:2 * D + lo + cw])
        u = c_gate * xh
        ext = jnp.concatenate([carry_ref[:, lo:lo + cw], u], axis=0)
        carry_ref[:, lo:lo + cw] = u[tm - SUBLANES:, :]
        conv = (wc_ref[0:1, lo:lo + cw] * _shifted(ext, 2, SUBLANES, tm)
                + wc_ref[1:2, lo:lo + cw] * _shifted(ext, 1, SUBLANES, tm)
                + wc_ref[2:3, lo:lo + cw] * u)
        acc = acc + _dot((b_gate * conv).astype(BF16), wout_ref[lo:lo + cw, :])
    o_ref[...] = x + mod_ref[2:3, :] * acc


def _sc_mixer(x, mod, g, w_in, w_conv, w_out, *, tm=512, cw=256):
    s = x.shape[0]
    return pl.pallas_call(
        functools.partial(_sc_kernel, tm=tm, cw=cw),
        grid=(s // tm,),
        in_specs=[_rows(tm, D), _resident((6, D)), _resident((1, D)), _resident((D, 3 * D)),
                  _resident((3, D)), _resident((D, D))],
        out_specs=_rows(tm, D),
        out_shape=jax.ShapeDtypeStruct((s, D), F32),
        scratch_shapes=[pltpu.VMEM((SUBLANES, D), F32)],
        compiler_params=_params("arbitrary"),
        name="sc_mixer",
    )(x, mod, g.reshape(1, D), w_in, w_conv, w_out)


CF_HALO = 32


def _cf_kernel(x_ref, mod_ref, g_ref, win_ref, wdw_ref, bdw_ref, lng_ref, lnb_ref, wout_ref,
               o_ref, carry_ref, u_ref, *, tm, cw):
    @pl.when(pl.program_id(0) == 0)
    def _():
        carry_ref[...] = jnp.zeros_like(carry_ref)

    x = x_ref[...]
    h = _rms_mod(x, g_ref[...], mod_ref[1:2, :], mod_ref[0:1, :]).astype(BF16)
    for c in range(D // cw):
        lo = c * cw
        a = _dot(h, win_ref[:, lo:lo + cw])
        gate = _dot(h, win_ref[:, D + lo:D + lo + cw])
        glu = a * jax.nn.sigmoid(gate)
        ext = jnp.concatenate([carry_ref[:, lo:lo + cw], glu], axis=0)
        carry_ref[:, lo:lo + cw] = glu[tm - CF_HALO:, :]
        conv = jnp.zeros((tm, cw), F32) + bdw_ref[:, lo:lo + cw]
        for b in range(SUBLANES):
            r = pltpu.roll(ext, b, axis=0) if b else ext
            for a8 in range(CF_HALO // SUBLANES):
                s = a8 * SUBLANES + b
                if s < CONFORMER_K:
                    k = CONFORMER_K - 1 - s
                    start = CF_HALO - a8 * SUBLANES
                    conv = conv + wdw_ref[k:k + 1, lo:lo + cw] * r[start:start + tm, :]
        u_ref[:, lo:lo + cw] = conv
    u = u_ref[...]
    uc = u - jnp.mean(u, axis=-1, keepdims=True)
    y = uc * lax.rsqrt(jnp.mean(uc * uc, axis=-1, keepdims=True) + LN_EPS)
    y = y * lng_ref[...] + lnb_ref[...]
    y = (y * jax.nn.sigmoid(y)).astype(BF16)
    o_ref[...] = x + mod_ref[2:3, :] * _dot(y, wout_ref[...])


def _cf_mixer(x, mod, g, w_in, w_dw, b_dw, ln_g, ln_b, w_out, *, tm=512, cw=256):
    s = x.shape[0]
    return pl.pallas_call(
        functools.partial(_cf_kernel, tm=tm, cw=cw),
        grid=(s // tm,),
        in_specs=[_rows(tm, D), _resident((6, D)), _resident((1, D)), _resident((D, 2 * D)),
                  _resident((CONFORMER_K, D)), _resident((1, D)), _resident((1, D)),
                  _resident((1, D)), _resident((D, D))],
        out_specs=_rows(tm, D),
        out_shape=jax.ShapeDtypeStruct((s, D), F32),
        scratch_shapes=[pltpu.VMEM((CF_HALO, D), F32), pltpu.VMEM((tm, D), F32)],
        compiler_params=_params("arbitrary"),
        name="cf_mixer",
    )(x, mod, g.reshape(1, D), w_in, w_dw, b_dw.reshape(1, D), ln_g.reshape(1, D),
      ln_b.reshape(1, D), w_out)


def _group_mean_matrix(n):
    r = lax.broadcasted_iota(jnp.int32, (n, n), 0) // HEAD_DIM
    c = lax.broadcasted_iota(jnp.int32, (n, n), 1) // HEAD_DIM
    return jnp.where(r == c, 1.0 / HEAD_DIM, 0.0).astype(BF16)


def _head_rms(z, gm):
    hi, lo = _split2(z * z)
    ms = _dot(hi, gm) + _dot(lo, gm)
    return z * lax.rsqrt(ms + RMS_EPS)


def _store_v_with_ones(v, v0_ref, v1_ref, lo, cw):
    lane = lax.broadcasted_iota(jnp.int32, v.shape, 1)
    first = (lane & (LANES - 1)) < HEAD_DIM
    v0_ref[:, lo:lo + cw] = jnp.where(first, v, 1.0).astype(BF16)
    v1_ref[:, lo:lo + cw] = jnp.where(first, 1.0, v).astype(BF16)


def _fox_proj_kernel(x_ref, mod_ref, g_ref, w_ref, bf_ref, qn_ref, kn_ref,
                     q_ref, k_ref, v0_ref, v1_ref, kx_ref, f_ref, carry_ref, tri_ref, *, tm, cw):
    @pl.when(pl.program_id(0) == 0)
    def _():
        carry_ref[...] = jnp.zeros_like(carry_ref)
        r = lax.broadcasted_iota(jnp.int32, (tm, tm), 0)
        c = lax.broadcasted_iota(jnp.int32, (tm, tm), 1)
        tri_ref[...] = jnp.where(r >= c, 1.0, 0.0).astype(BF16)

    h = _rms_mod(x_ref[...], g_ref[...], mod_ref[1:2, :], mod_ref[0:1, :]).astype(BF16)
    gm = _group_mean_matrix(cw)
    for c in range(D // cw):
        lo = c * cw
        q = _head_rms(_dot(h, w_ref[:, lo:lo + cw]), gm)
        q_ref[:, lo:lo + cw] = (q * qn_ref[:, lo:lo + cw] * (SCALE * LOG2E)).astype(BF16)
        k = _head_rms(_dot(h, w_ref[:, D + lo:D + lo + cw]), gm)
        k_ref[:, lo:lo + cw] = (k * kn_ref[:, lo:lo + cw]).astype(BF16)
        _store_v_with_ones(_dot(h, w_ref[:, 2 * D + lo:2 * D + lo + cw]), v0_ref, v1_ref, lo, cw)
    fl = _dot(h, w_ref[:, 3 * D:3 * D + LANES]) + bf_ref[...]
    lf = jnp.minimum(fl, 0.0) - jnp.log1p(jnp.exp(-jnp.abs(fl)))
    tri = tri_ref[...]
    hi, mid, lo3 = _split3(lf)
    local = _dot(tri, hi) + _dot(tri, mid) + _dot(tri, lo3)
    cum = local + carry_ref[0:1, :]
    f_ref[...] = cum
    carry_ref[...] = jnp.broadcast_to(cum[tm - 1:tm, :], carry_ref.shape)
    head = lax.broadcasted_iota(jnp.int32, (LANES, D), 0)
    lane = lax.broadcasted_iota(jnp.int32, (LANES, D), 1)
    target = lax.shift_right_logical(head, 1) * LANES + (head & 1) * 3
    kx = jnp.zeros((tm, D), F32)
    for t, piece in enumerate(_split3(local * (-LOG2E))):
        place = jnp.where((lane == target + t) & (head < N_HEADS), 1.0, 0.0).astype(BF16)
        kx = kx + _dot(piece, place)
    kx_ref[...] = kx.astype(BF16)


def _fox_proj(x, mod, g, w_in, b_f, qn, kn, *, tm=ATTN_TILE, cw=256):
    s = x.shape[0]
    w = jnp.pad(w_in, ((0, 0), (0, LANES - N_HEADS))).astype(BF16)
    bf = jnp.pad(b_f, (0, LANES - N_HEADS)).reshape(1, LANES)
    return pl.pallas_call(
        functools.partial(_fox_proj_kernel, tm=tm, cw=cw),
        grid=(s // tm,),
        in_specs=[_rows(tm, D), _resident((6, D)), _resident((1, D)),
                  _resident((D, 3 * D + LANES)), _resident((1, LANES)),
                  _resident((1, D)), _resident((1, D))],
        out_specs=[_rows(tm, D)] * 5 + [_rows(tm, LANES)],
        out_shape=[jax.ShapeDtypeStruct((s, D), BF16)] * 5 + [jax.ShapeDtypeStruct((s, LANES), F32)],
        scratch_shapes=[pltpu.VMEM((SUBLANES, LANES), F32), pltpu.VMEM((tm, tm), BF16)],
        compiler_params=_params("arbitrary"),
        name="fox_proj",
    )(x, mod, g.reshape(1, D), w, bf, jnp.tile(qn, N_HEADS).reshape(1, D),
      jnp.tile(kn, N_HEADS).reshape(1, D))


def _moba_proj_kernel(x_ref, mod_ref, g_ref, w_ref, qn_ref, kn_ref,
                      q_ref, k_ref, v0_ref, v1_ref, qf_ref, km_ref, *, tm, cw):
    h = _rms_mod(x_ref[...], g_ref[...], mod_ref[1:2, :], mod_ref[0:1, :]).astype(BF16)
    gm = _group_mean_matrix(cw)
    for c in range(D // cw):
        lo = c * cw
        q = _head_rms(_dot(h, w_ref[:, lo:lo + cw]), gm) * qn_ref[:, lo:lo + cw]
        qf_ref[:, lo:lo + cw] = q
        q_ref[:, lo:lo + cw] = (q * (SCALE * LOG2E)).astype(BF16)
        k = _head_rms(_dot(h, w_ref[:, D + lo:D + lo + cw]), gm) * kn_ref[:, lo:lo + cw]
        k_ref[:, lo:lo + cw] = k.astype(BF16)
        for b in range(tm // MOBA_BLOCK):
            km_ref[b, :, lo:lo + cw] = jnp.mean(
                k[b * MOBA_BLOCK:(b + 1) * MOBA_BLOCK, :], axis=0, keepdims=True)
        _store_v_with_ones(_dot(h, w_ref[:, 2 * D + lo:2 * D + lo + cw]), v0_ref, v1_ref, lo, cw)


def _moba_proj(x, mod, g, w_in, qn, kn, *, tm=512, cw=256):
    s = x.shape[0]
    nb = tm // MOBA_BLOCK
    return pl.pallas_call(
        functools.partial(_moba_proj_kernel, tm=tm, cw=cw),
        grid=(s // tm,),
        in_specs=[_rows(tm, D), _resident((6, D)), _resident((1, D)), _resident((D, 3 * D)),
                  _resident((1, D)), _resident((1, D))],
        out_specs=[_rows(tm, D)] * 5 + [pl.BlockSpec((nb, 1, D), lambda i: (i, 0, 0))],
        out_shape=[jax.ShapeDtypeStruct((s, D), BF16)] * 4
        + [jax.ShapeDtypeStruct((s, D), F32), jax.ShapeDtypeStruct((s // MOBA_BLOCK, 1, D), F32)],
        compiler_params=_params("parallel"),
        name="moba_proj",
    )(x, mod, g.reshape(1, D), w_in, jnp.tile(qn, N_HEADS).reshape(1, D),
      jnp.tile(kn, N_HEADS).reshape(1, D))


MAX_KEY_BLOCKS = HEAD_DIM


def _moba_gate_kernel(q_ref, km_ref, slope_ref, qx_ref):
    own = pl.program_id(0)
    t = q_ref.shape[0]
    blk = lax.broadcasted_iota(jnp.int32, (t, LANES), 1)
    is_block_lane = blk < MAX_KEY_BLOCKS
    zeros_km = jnp.zeros((LANES - MAX_KEY_BLOCKS, MXU_W), BF16)
    for g4 in range(D // MXU_W):
        lo = g4 * MXU_W
        q = q_ref[:, lo:lo + MXU_W]
        q_lane_head = lax.shift_right_logical(
            lax.broadcasted_iota(jnp.int32, (t, MXU_W), 1), HEAD_DIM.bit_length() - 1)
        km_hi, km_lo = _split2(km_ref[:, lo:lo + MXU_W])
        rhs_hi = jnp.concatenate([km_hi, zeros_km], axis=0)
        rhs_lo = jnp.concatenate([km_lo, zeros_km], axis=0)
        for hl in range(MXU_W // HEAD_DIM):
            head = 4 * g4 + hl
            q_hi, q_lo = _split2(jnp.where(q_lane_head == hl, q, 0.0))
            gate = _dot_nt(q_hi, rhs_hi) + _dot_nt(q_hi, rhs_lo) + _dot_nt(q_lo, rhs_hi)
            keep = blk == own
            gh = jnp.where(is_block_lane, jnp.where(blk < own, gate, NEG_INF), -jnp.inf)
            for _ in range(MOBA_TOPK):
                mx = jnp.max(gh, axis=1, keepdims=True)
                idx = jnp.min(jnp.where(gh == mx, blk, LANES), axis=1, keepdims=True)
                pick = blk == idx
                keep = keep | (pick & (idx < own))
                gh = jnp.where(pick, -jnp.inf, gh)
            qx = jnp.where(is_block_lane, jnp.where(keep, 0.0, NEG_INF), slope_ref[head:head + 1, :])
            qx_ref[:, head * LANES:(head + 1) * LANES] = qx.astype(BF16)


def _moba_gate(qf, kmean, slope_lanes):
    s = qf.shape[0]
    n_kb = s // MOBA_BLOCK
    assert n_kb <= MAX_KEY_BLOCKS
    km = jnp.pad(kmean.reshape(n_kb, D), ((0, MAX_KEY_BLOCKS - n_kb), (0, 0)))
    return pl.pallas_call(
        _moba_gate_kernel,
        grid=(n_kb,),
        in_specs=[_rows(MOBA_BLOCK, D), _resident((MAX_KEY_BLOCKS, D)), _resident((N_HEADS, LANES))],
        out_specs=_rows(MOBA_BLOCK, N_HEADS * LANES),
        out_shape=jax.ShapeDtypeStruct((s, N_HEADS * LANES), BF16),
        compiler_params=_params("parallel"),
        name="moba_gate",
    )(qf, km, slope_lanes)


def _attn_kernel(*refs, t, moba):
    if moba:
        nvis_ref, r_ref, q_ref, kk_ref, kx_ref, v0_ref, v1_ref, qx_ref, o_ref, acc_ref, m_ref = refs
    else:
        nvis_ref, r_ref, q_ref, kk_ref, kx_ref, v0_ref, v1_ref, o_ref, acc_ref, m_ref = refs
    pair = pl.program_id(0)
    i = pl.program_id(1)
    q = q_ref[...]
    lane = lax.broadcasted_iota(jnp.int32, (t, LANES), 1)
    row = lax.broadcasted_iota(jnp.int32, (t, t), 0)
    col = lax.broadcasted_iota(jnp.int32, (t, t), 1)
    outs = []
    for hh, v_ref in enumerate((v0_ref, v1_ref)):
        h = 2 * pair + hh
        mine = (lane < HEAD_DIM) if hh == 0 else (lane >= HEAD_DIM)
        if moba:
            qx = qx_ref[:, hh * LANES:(hh + 1) * LANES]
        else:
            qx = jnp.where((lane >= 3 * hh) & (lane < 3 * hh + 3), 1.0, 0.0).astype(BF16)
        qh = jnp.concatenate([jnp.where(mine, q, jnp.zeros_like(q)), qx], axis=1)

        def tile(j):
            start = pl.multiple_of(j * t, t)
            kk = jnp.concatenate([kk_ref[pl.ds(start, t), :], kx_ref[pl.ds(start, t), :]], axis=1)
            return _dot_nt(qh, kk), v_ref[pl.ds(start, t), :]

        def across(stat):
            return jnp.concatenate([stat] * (t // LANES), axis=1)

        s, vv = tile(i)
        s = jnp.where(col <= row, s, NEG_INF)
        m = jnp.broadcast_to(jnp.max(s, axis=1, keepdims=True), (t, LANES))
        acc_ref[hh] = _dot(jnp.exp2(s - across(m)).astype(BF16), vv)
        m_ref[hh] = m

        def body(jj, carry):
            j = i - 1 - jj
            if moba:
                dist = jnp.zeros((1, 1), jnp.int32) + (i - j) * t
                c = -(r_ref[h, 0] * dist.astype(F32))
            else:
                c = (jnp.zeros((1, 1), F32) + r_ref[h, i]) - (jnp.zeros((1, 1), F32) + r_ref[h, j])
            s, vv = tile(j)
            m_old = m_ref[hh]
            m_tile = jnp.broadcast_to(jnp.max(s, axis=1, keepdims=True), (t, LANES))
            m_new = jnp.maximum(m_old, m_tile + c)
            alpha = jnp.exp2(m_old - m_new)
            p = jnp.exp2(s - across(m_new - c))
            acc_ref[hh] = alpha * acc_ref[hh] + _dot(p.astype(BF16), vv)
            m_ref[hh] = m_new
            return carry

        lax.fori_loop(0, nvis_ref[h, i], body, 0)
        acc = acc_ref[hh]
        outs.append(acc * (1.0 / pltpu.roll(acc, HEAD_DIM, axis=1)))
    o_ref[...] = jnp.where(lane < HEAD_DIM, outs[0], outs[1]).astype(BF16)


def _attention(nvis, r, q, kk, kx, v0, v1, qx=None, *, t):
    s = q.shape[0]
    moba = qx is not None
    col_block = pl.BlockSpec((s, LANES), lambda p, i, *_: (0, p), pipeline_mode=pl.Buffered(1))
    tile = pl.BlockSpec((t, LANES), lambda p, i, *_: (i, p))
    ins = [q, kk, kx, v0, v1]
    if moba:
        kx_spec = pl.BlockSpec((s, LANES), lambda p, i, *_: (0, 0), pipeline_mode=pl.Buffered(1))
        specs = [tile, col_block, kx_spec, col_block, col_block,
                 pl.BlockSpec((t, 2 * LANES), lambda p, i, *_: (i, p))]
        ins.append(qx)
    else:
        specs = [tile, col_block, col_block, col_block, col_block]
    return pl.pallas_call(
        functools.partial(_attn_kernel, t=t, moba=moba),
        grid_spec=pltpu.PrefetchScalarGridSpec(
            num_scalar_prefetch=2,
            grid=(N_PAIRS, s // t),
            in_specs=specs,
            out_specs=tile,
            scratch_shapes=[pltpu.VMEM((2, t, LANES), F32), pltpu.VMEM((2, t, LANES), F32)],
        ),
        out_shape=jax.ShapeDtypeStruct((s, D), BF16),
        compiler_params=_params("parallel", "parallel"),
        name="moba_attn" if moba else "fox_attn",
    )(nvis, r, *ins)


def _first_visible(vis):
    nt = vis.shape[1]
    i = jnp.arange(nt)[None, :, None]
    j = jnp.arange(nt)[None, None, :]
    first = jnp.min(jnp.where(vis & (j < i), j, i), axis=2)
    return (i[:, :, 0] - first).astype(jnp.int32)


def _score_bound(qn, kn):
    return 1.02 * HEAD_DIM ** 0.5 * jnp.max(jnp.abs(qn)) * jnp.max(jnp.abs(kn))


def _fox_schedule(cum_f, qn, kn, t):
    f = cum_f[:, :N_HEADS].T
    f_before = jnp.concatenate([jnp.zeros((N_HEADS, 1), F32), f[:, t - 1:-1:t]], axis=1)
    reach = SKIP_THRESHOLD + 2.0 * _score_bound(qn, kn)
    vis = (f[:, 0::t, None] - f[:, None, t - 1::t]) >= -reach
    return _first_visible(vis), LOG2E * f_before


def _moba_constants(s, qn, kn, t):
    slopes = jnp.exp2(-ALIBI_MAX_EXP * jnp.arange(1, N_HEADS + 1, dtype=F32) / N_HEADS)
    hi, mid, lo = _split3(slopes * LOG2E)
    pieces = jnp.stack([hi, mid, lo, hi, mid, lo], axis=1).astype(F32)
    slope_lanes = jnp.zeros((N_HEADS, LANES), F32).at[:, MAX_KEY_BLOCKS:MAX_KEY_BLOCKS + 6].set(pieces)
    pos = jnp.arange(s)
    off = pos % t
    rem = (off % MOBA_BLOCK).astype(F32)[:, None]
    base = (off - off % MOBA_BLOCK).astype(F32)[:, None]
    onehot = (pos[:, None] // MOBA_BLOCK == jnp.arange(MAX_KEY_BLOCKS)[None, :]).astype(F32)
    kx = jnp.concatenate([onehot, rem, rem, rem, base, base, base,
                          jnp.zeros((s, LANES - MAX_KEY_BLOCKS - 6), F32)], axis=1).astype(BF16)
    nt = s // t
    i = jnp.arange(nt)[None, :, None]
    j = jnp.arange(nt)[None, None, :]
    nearest = ((i - j - 1) * t + 1).astype(F32)
    reach = SKIP_THRESHOLD + 2.0 * _score_bound(qn, kn)
    nvis = _first_visible(slopes[:, None, None] * nearest <= reach)
    r = jnp.broadcast_to((slopes * LOG2E)[:, None], (N_HEADS, nt))
    return slope_lanes, kx, nvis, r


def kernel(x, c, ln_mix_g, ln_ffn_g, ada_w, ada_b, ffn_up, ffn_conv, ffn_down, sc_in, sc_conv, sc_out, cf_in, cf_dw, cf_dw_b, cf_ln_g, cf_ln_b, cf_out, fox_in, fox_bf, fox_qn, fox_kn, fox_out, moba_in, moba_qn, moba_kn, moba_out):
    batch, s, _ = x.shape
    assert batch == 1 and c.shape[0] == 1
    depth = ada_w.shape[0]
    mods = _ada(c, ada_w, ada_b)
    xs = x[0]
    for i in range(depth):
        kind, j = i % 4, i // 4
        mod = mods[i]
        pre = None
        if kind == 0:
            xs = _sc_mixer(xs, mod, ln_mix_g[i], sc_in[j].astype(BF16), sc_conv[j],
                           sc_out[j].astype(BF16))
        elif kind == 1:
            xs = _cf_mixer(xs, mod, ln_mix_g[i], cf_in[j].astype(BF16), cf_dw[j], cf_dw_b[j],
                           cf_ln_g[j], cf_ln_b[j], cf_out[j].astype(BF16))
        elif kind == 2:
            q, k, v0, v1, kx, cum_f = _fox_proj(xs, mod, ln_mix_g[i], fox_in[j], fox_bf[j],
                                                fox_qn[j], fox_kn[j])
            nvis, r = _fox_schedule(cum_f, fox_qn[j], fox_kn[j], ATTN_TILE)
            pre = (_attention(nvis, r, q, k, kx, v0, v1, t=ATTN_TILE), fox_out[j].astype(BF16))
        else:
            q, k, v0, v1, qf, kmean = _moba_proj(xs, mod, ln_mix_g[i], moba_in[j].astype(BF16),
                                                 moba_qn[j], moba_kn[j])
            slope_lanes, kx, nvis, r = _moba_constants(s, moba_qn[j], moba_kn[j], ATTN_TILE)
            qx = _moba_gate(qf, kmean, slope_lanes)
            pre = (_attention(nvis, r, q, k, kx, v0, v1, qx, t=ATTN_TILE), moba_out[j].astype(BF16))
        xs = _ffn(xs, mod, ln_ffn_g[i], ffn_up[i].astype(BF16), ffn_conv[i],
                  ffn_down[i].astype(BF16), pre)
    return xs[None]
```

```python
import functools

import jax
import jax.numpy as jnp
from jax import lax
from jax.experimental import pallas as pl
from jax.experimental.pallas import tpu as pltpu

D = 1024
N_HEADS = 16
HEAD_DIM = D // N_HEADS
N_PAIRS = N_HEADS // 2
D_FF = ((8 * D // 3 + 127) // 128) * 128
CONFORMER_K = 31
MOBA_BLOCK = 256
MOBA_TOPK = 3
ALIBI_MAX_EXP = 8.0
RMS_EPS = 1e-6
LN_EPS = 1e-5
NEG_INF = -1e30
SCALE = HEAD_DIM ** -0.5
LOG2E = 1.4426950408889634
SKIP_THRESHOLD = 106.0
ATTN_TILE = 512
MAX_KEY_BLOCKS = 64

LANES = 128
SUBLANES = 8
MXU_W = 256
VMEM_LIMIT = 56 * 1024 * 1024

F32 = jnp.float32
BF16 = jnp.bfloat16

_dot = functools.partial(jnp.dot, preferred_element_type=F32)


def _dot_nt(a, b):
    return lax.dot_general(a, b, (((1,), (1,)), ((), ())), preferred_element_type=F32)


def _resident(shape):
    nd = len(shape)
    return pl.BlockSpec(shape, lambda *_: (0,) * nd, pipeline_mode=pl.Buffered(1))


def _rows(tm, width):
    return pl.BlockSpec((tm, width), lambda i: (i, 0))


def _params(*sem):
    return pltpu.CompilerParams(dimension_semantics=sem, vmem_limit_bytes=VMEM_LIMIT)


def _rms_mod(x, g, scale, shift):
    ms = jnp.mean(x * x, axis=-1, keepdims=True)
    return (x * lax.rsqrt(ms + RMS_EPS) * g) * (1.0 + scale) + shift


def _shifted(ext, s, halo, tm):
    a, b = divmod(s, SUBLANES)
    r = pltpu.roll(ext, b, axis=0) if b else ext
    lo = halo - a * SUBLANES
    return r[lo:lo + tm, :]


def _split2(z):
    hi = z.astype(BF16)
    lo = (z - hi.astype(F32)).astype(BF16)
    return hi, lo


def _split3(z):
    hi = z.astype(BF16)
    r = z - hi.astype(F32)
    mid = r.astype(BF16)
    lo = (r - mid.astype(F32)).astype(BF16)
    return hi, mid, lo


def _ada_kernel(c_ref, w_ref, b_ref, o_ref):
    cond = jax.nn.silu(c_ref[...])
    o_ref[0, 0] = jnp.sum(w_ref[0] * cond, axis=0, keepdims=True) + b_ref[0, 0]


def _ada(c, ada_w, ada_b):
    depth = ada_w.shape[0]
    out = pl.pallas_call(
        _ada_kernel,
        grid=(depth, 6),
        in_specs=[
            pl.BlockSpec((D, 1), lambda l, j: (0, 0)),
            pl.BlockSpec((1, D, D), lambda l, j: (l, 0, j)),
            pl.BlockSpec((1, 1, 1, D), lambda l, j: (l, j, 0, 0)),
        ],
        out_specs=pl.BlockSpec((1, 1, 1, D), lambda l, j: (l, j, 0, 0)),
        out_shape=jax.ShapeDtypeStruct((depth, 6, 1, D), F32),
        compiler_params=_params("parallel", "parallel"),
        name="ada",
    )(c.reshape(D, 1), ada_w, ada_b.reshape(depth, 6, 1, D))
    return out.reshape(depth, 6, D)


def _ffn_kernel(*refs, tm, cw, pre_proj):
    if pre_proj:
        x_ref, a_ref, wo_ref, mod_ref, g_ref, wup_ref, wc_ref, wdn_ref, o_ref, carry_ref = refs
    else:
        x_ref, mod_ref, g_ref, wup_ref, wc_ref, wdn_ref, o_ref, carry_ref = refs

    @pl.when(pl.program_id(0) == 0)
    def _():
        carry_ref[...] = jnp.zeros_like(carry_ref)

    x = x_ref[...]
    if pre_proj:
        x = x + mod_ref[2:3, :] * _dot(a_ref[...], wo_ref[...])
    h = _rms_mod(x, g_ref[...], mod_ref[4:5, :], mod_ref[3:4, :]).astype(BF16)
    acc = jnp.zeros((tm, D), F32)
    for c in range(D_FF // cw):
        lo = c * cw
        u = _dot(h, wup_ref[:, lo:lo + cw])
        gate = _dot(h, wup_ref[:, D_FF + lo:D_FF + lo + cw])
        ext = jnp.concatenate([carry_ref[:, lo:lo + cw], u], axis=0)
        carry_ref[:, lo:lo + cw] = u[tm - SUBLANES:, :]
        conv = (wc_ref[0:1, lo:lo + cw] * _shifted(ext, 2, SUBLANES, tm)
                + wc_ref[1:2, lo:lo + cw] * _shifted(ext, 1, SUBLANES, tm)
                + wc_ref[2:3, lo:lo + cw] * u)
        act = (conv * jax.nn.sigmoid(conv)) * gate
        acc = acc + _dot(act.astype(BF16), wdn_ref[lo:lo + cw, :])
    o_ref[...] = x + mod_ref[5:6, :] * acc


def _ffn(x, mod, g, w_up, w_conv, w_down, pre=None, *, tm=512, cw=256):
    s = x.shape[0]
    ins, specs = [x], [_rows(tm, D)]
    if pre is not None:
        a, w_o = pre
        ins += [a, w_o]
        specs += [_rows(tm, D), _resident((D, D))]
    ins += [mod, g.reshape(1, D), w_up, w_conv, w_down]
    specs += [_resident((6, D)), _resident((1, D)), _resident((D, 2 * D_FF)),
              _resident((3, D_FF)), _resident((D_FF, D))]
    return pl.pallas_call(
        functools.partial(_ffn_kernel, tm=tm, cw=cw, pre_proj=pre is not None),
        grid=(s // tm,),
        in_specs=specs,
        out_specs=_rows(tm, D),
        out_shape=jax.ShapeDtypeStruct((s, D), F32),
        scratch_shapes=[pltpu.VMEM((SUBLANES, D_FF), F32)],
        compiler_params=_params("arbitrary"),
        name="ffn_proj" if pre is not None else "ffn",
    )(*ins)


def _sc_kernel(x_ref, mod_ref, g_ref, win_ref, wc_ref, wout_ref, o_ref, carry_ref, *, tm, cw):
    @pl.when(pl.program_id(0) == 0)
    def _():
        carry_ref[...] = jnp.zeros_like(carry_ref)

    x = x_ref[...]
    h = _rms_mod(x, g_ref[...], mod_ref[1:2, :], mod_ref[0:1, :]).astype(BF16)
    acc = jnp.zeros((tm, D), F32)
    for c in range(D // cw):
        lo = c * cw
        b_gate = _dot(h, win_ref[:, lo:lo + cw])
        c_gate = _dot(h, win_ref[:, D + lo:D + lo + cw])
        xh = _dot(h, win_ref[:, 2 * D + lo:2 * D + lo + cw])
        u = c_gate * xh
        ext = jnp.concatenate([carry_ref[:, lo:lo + cw], u], axis=0)
        carry_ref[:, lo:lo + cw] = u[tm - SUBLANES:, :]
        conv = (wc_ref[0:1, lo:lo + cw] * _shifted(ext, 2, SUBLANES, tm)
                + wc_ref[1:2, lo:lo + cw] * _shifted(ext, 1, SUBLANES, tm)
                + wc_ref[2:3, lo:lo + cw] * u)
        acc = acc + _dot((b_gate * conv).astype(BF16), wout_ref[lo:lo + cw, :])
    o_ref[...] = x + mod_ref[2:3, :] * acc


def _sc_mixer(x, mod, g, w_in, w_conv, w_out, *, tm=512, cw=256):
    s = x.shape[0]
    return pl.pallas_call(
        functools.partial(_sc_kernel, tm=tm, cw=cw),
        grid=(s // tm,),
        in_specs=[_rows(tm, D), _resident((6, D)), _resident((1, D)), _resident((D, 3 * D)),
                  _resident((3, D)), _resident((D, D))],
        out_specs=_rows(tm, D),
        out_shape=jax.ShapeDtypeStruct((s, D), F32),
        scratch_shapes=[pltpu.VMEM((SUBLANES, D), F32)],
        compiler_params=_params("arbitrary"),
        name="sc_mixer",
    )(x, mod, g.reshape(1, D), w_in, w_conv, w_out)


CF_HALO = 32


def _cf_kernel(x_ref, mod_ref, g_ref, win_ref, wdw_ref, bdw_ref, lng_ref, lnb_ref, wout_ref,
               o_ref, carry_ref, u_ref, *, tm, cw):
    @pl.when(pl.program_id(0) == 0)
    def _():
        carry_ref[...] = jnp.zeros_like(carry_ref)

    x = x_ref[...]
    h = _rms_mod(x, g_ref[...], mod_ref[1:2, :], mod_ref[0:1, :]).astype(BF16)
    for c in range(D // cw):
        lo = c * cw
        a = _dot(h, win_ref[:, lo:lo + cw])
        gate = _dot(h, win_ref[:, D + lo:D + lo + cw])
        glu = a * jax.nn.sigmoid(gate)
        ext = jnp.concatenate([carry_ref[:, lo:lo + cw], glu], axis=0)
        carry_ref[:, lo:lo + cw] = glu[tm - CF_HALO:, :]
        conv = jnp.zeros((tm, cw), F32) + bdw_ref[:, lo:lo + cw]
        for b in range(SUBLANES):
            r = pltpu.roll(ext, b, axis=0) if b else ext
            for a8 in range(CF_HALO // SUBLANES):
                s = a8 * SUBLANES + b
                if s < CONFORMER_K:
                    k = CONFORMER_K - 1 - s
                    start = CF_HALO - a8 * SUBLANES
                    conv = conv + wdw_ref[k:k + 1, lo:lo + cw] * r[start:start + tm, :]
        u_ref[:, lo:lo + cw] = conv
    u = u_ref[...]
    uc = u - jnp.mean(u, axis=-1, keepdims=True)
    y = uc * lax.rsqrt(jnp.mean(uc * uc, axis=-1, keepdims=True) + LN_EPS)
    y = y * lng_ref[...] + lnb_ref[...]
    y = (y * jax.nn.sigmoid(y)).astype(BF16)
    o_ref[...] = x + mod_ref[2:3, :] * _dot(y, wout_ref[...])


def _cf_mixer(x, mod, g, w_in, w_dw, b_dw, ln_g, ln_b, w_out, *, tm=512, cw=256):
    s = x.shape[0]
    return pl.pallas_call(
        functools.partial(_cf_kernel, tm=tm, cw=cw),
        grid=(s // tm,),
        in_specs=[_rows(tm, D), _resident((6, D)), _resident((1, D)), _resident((D, 2 * D)),
                  _resident((CONFORMER_K, D)), _resident((1, D)), _resident((1, D)),
                  _resident((1, D)), _resident((D, D))],
        out_specs=_rows(tm, D),
        out_shape=jax.ShapeDtypeStruct((s, D), F32),
        scratch_shapes=[pltpu.VMEM((CF_HALO, D), F32), pltpu.VMEM((tm, D), F32)],
        compiler_params=_params("arbitrary"),
        name="cf_mixer",
    )(x, mod, g.reshape(1, D), w_in, w_dw, b_dw.reshape(1, D), ln_g.reshape(1, D),
      ln_b.reshape(1, D), w_out)


def _group_mean_matrix(n):
    r = lax.broadcasted_iota(jnp.int32, (n, n), 0) // HEAD_DIM
    c = lax.broadcasted_iota(jnp.int32, (n, n), 1) // HEAD_DIM
    return jnp.where(r == c, 1.0 / HEAD_DIM, 0.0).astype(BF16)


def _head_rms(z, gm):
    hi, lo = _split2(z * z)
    ms = _dot(hi, gm) + _dot(lo, gm)
    return z * lax.rsqrt(ms + RMS_EPS)


def _store_v_with_ones(v, v0_ref, v1_ref, lo, cw):
    lane = lax.broadcasted_iota(jnp.int32, v.shape, 1)
    first = (lane & (LANES - 1)) < HEAD_DIM
    v0_ref[:, lo:lo + cw] = jnp.where(first, v, 1.0).astype(BF16)
    v1_ref[:, lo:lo + cw] = jnp.where(first, 1.0, v).astype(BF16)


def _fox_proj_kernel(x_ref, mod_ref, g_ref, w_ref, bf_ref, qn_ref, kn_ref,
                     q_ref, k_ref, v0_ref, v1_ref, kx_ref, f_ref, carry_ref, tri_ref, *, tm, cw):
    @pl.when(pl.program_id(0) == 0)
    def _():
        carry_ref[...] = jnp.zeros_like(carry_ref)
        r = lax.broadcasted_iota(jnp.int32, (tm, tm), 0)
        c = lax.broadcasted_iota(jnp.int32, (tm, tm), 1)
        tri_ref[...] = jnp.where(r >= c, 1.0, 0.0).astype(BF16)

    h = _rms_mod(x_ref[...], g_ref[...], mod_ref[1:2, :], mod_ref[0:1, :]).astype(BF16)
    gm = _group_mean_matrix(cw)
    for c in range(D // cw):
        lo = c * cw
        q = _head_rms(_dot(h, w_ref[:, lo:lo + cw]), gm)
        q_ref[:, lo:lo + cw] = (q * qn_ref[:, lo:lo + cw] * (SCALE * LOG2E)).astype(BF16)
        k = _head_rms(_dot(h, w_ref[:, D + lo:D + lo + cw]), gm)
        k_ref[:, lo:lo + cw] = (k * kn_ref[:, lo:lo + cw]).astype(BF16)
        _store_v_with_ones(_dot(h, w_ref[:, 2 * D + lo:2 * D + lo + cw]), v0_ref, v1_ref, lo, cw)
    fl = _dot(h, w_ref[:, 3 * D:3 * D + LANES]) + bf_ref[...]
    lf = jnp.minimum(fl, 0.0) - jnp.log1p(jnp.exp(-jnp.abs(fl)))
    tri = tri_ref[...]
    hi, mid, lo3 = _split3(lf)
    local = _dot(tri, hi) + _dot(tri, mid) + _dot(tri, lo3)
    cum = local + carry_ref[0:1, :]
    f_ref[...] = cum
    carry_ref[...] = jnp.broadcast_to(cum[tm - 1:tm, :], carry_ref.shape)
    head = lax.broadcasted_iota(jnp.int32, (LANES, D), 0)
    lane = lax.broadcasted_iota(jnp.int32, (LANES, D), 1)
    target = lax.shift_right_logical(head, 1) * LANES + (head & 1) * 3
    kx = jnp.zeros((tm, D), F32)
    for t, piece in enumerate(_split3(local * (-LOG2E))):
        place = jnp.where((lane == target + t) & (head < N_HEADS), 1.0, 0.0).astype(BF16)
        kx = kx + _dot(piece, place)
    kx_ref[...] = kx.astype(BF16)


def _fox_proj(x, mod, g, w_in, b_f, qn, kn, *, tm=ATTN_TILE, cw=256):
    s = x.shape[0]
    w = jnp.pad(w_in, ((0, 0), (0, LANES - N_HEADS))).astype(BF16)
    bf = jnp.pad(b_f, (0, LANES - N_HEADS)).reshape(1, LANES)
    return pl.pallas_call(
        functools.partial(_fox_proj_kernel, tm=tm, cw=cw),
        grid=(s // tm,),
        in_specs=[_rows(tm, D), _resident((6, D)), _resident((1, D)),
                  _resident((D, 3 * D + LANES)), _resident((1, LANES)),
                  _resident((1, D)), _resident((1, D))],
        out_specs=[_rows(tm, D)] * 5 + [_rows(tm, LANES)],
        out_shape=[jax.ShapeDtypeStruct((s, D), BF16)] * 5 + [jax.ShapeDtypeStruct((s, LANES), F32)],
        scratch_shapes=[pltpu.VMEM((SUBLANES, LANES), F32), pltpu.VMEM((tm, tm), BF16)],
        compiler_params=_params("arbitrary"),
        name="fox_proj",
    )(x, mod, g.reshape(1, D), w, bf, jnp.tile(qn, N_HEADS).reshape(1, D),
      jnp.tile(kn, N_HEADS).reshape(1, D))


def _moba_proj_kernel(x_ref, mod_ref, g_ref, w_ref, qn_ref, kn_ref,
                      q_ref, k_ref, v0_ref, v1_ref, qf_ref, km_ref, kx_ref, *, tm, cw):
    row = lax.broadcasted_iota(jnp.int32, (tm, LANES), 0)
    lane = lax.broadcasted_iota(jnp.int32, (tm, LANES), 1)
    rem = row & (MOBA_BLOCK - 1)
    block = pl.program_id(0) * (tm // MOBA_BLOCK) + lax.shift_right_logical(row, MOBA_BLOCK.bit_length() - 1)
    offset_lanes = jnp.where(lane < MAX_KEY_BLOCKS + 3, rem, row - rem).astype(F32)
    kx_ref[...] = jnp.where(
        lane < MAX_KEY_BLOCKS, jnp.where(lane == block, 1.0, 0.0),
        jnp.where(lane < MAX_KEY_BLOCKS + 6, offset_lanes, 0.0)).astype(BF16)

    h = _rms_mod(x_ref[...], g_ref[...], mod_ref[1:2, :], mod_ref[0:1, :]).astype(BF16)
    gm = _group_mean_matrix(cw)
    for c in range(D // cw):
        lo = c * cw
        q = _head_rms(_dot(h, w_ref[:, lo:lo + cw]), gm) * qn_ref[:, lo:lo + cw]
        qf_ref[:, lo:lo + cw] = q
        q_ref[:, lo:lo + cw] = (q * (SCALE * LOG2E)).astype(BF16)
        k = _head_rms(_dot(h, w_ref[:, D + lo:D + lo + cw]), gm) * kn_ref[:, lo:lo + cw]
        k_ref[:, lo:lo + cw] = k.astype(BF16)
        for b in range(tm // MOBA_BLOCK):
            km_ref[b, :, lo:lo + cw] = jnp.mean(
                k[b * MOBA_BLOCK:(b + 1) * MOBA_BLOCK, :], axis=0, keepdims=True)
        _store_v_with_ones(_dot(h, w_ref[:, 2 * D + lo:2 * D + lo + cw]), v0_ref, v1_ref, lo, cw)


def _moba_proj(x, mod, g, w_in, qn, kn, *, tm=ATTN_TILE, cw=256):
    s = x.shape[0]
    nb = tm // MOBA_BLOCK
    return pl.pallas_call(
        functools.partial(_moba_proj_kernel, tm=tm, cw=cw),
        grid=(s // tm,),
        in_specs=[_rows(tm, D), _resident((6, D)), _resident((1, D)), _resident((D, 3 * D)),
                  _resident((1, D)), _resident((1, D))],
        out_specs=[_rows(tm, D)] * 5 + [pl.BlockSpec((nb, 1, D), lambda i: (i, 0, 0)),
                                        _rows(tm, LANES)],
        out_shape=[jax.ShapeDtypeStruct((s, D), BF16)] * 4
        + [jax.ShapeDtypeStruct((s, D), F32), jax.ShapeDtypeStruct((s // MOBA_BLOCK, 1, D), F32),
           jax.ShapeDtypeStruct((s, LANES), BF16)],
        compiler_params=_params("parallel"),
        name="moba_proj",
    )(x, mod, g.reshape(1, D), w_in, jnp.tile(qn, N_HEADS).reshape(1, D),
      jnp.tile(kn, N_HEADS).reshape(1, D))


def _moba_gate_kernel(q_ref, km_ref, slope_ref, qx_ref):
    own = pl.program_id(0)
    t = q_ref.shape[0]
    blk = lax.broadcasted_iota(jnp.int32, (LANES, t), 0)
    lane = lax.broadcasted_iota(jnp.int32, (t, LANES), 1)
    km_lane_head = lax.shift_right_logical(
        lax.broadcasted_iota(jnp.int32, (LANES, MXU_W), 1), HEAD_DIM.bit_length() - 1)
    zeros_km = jnp.zeros((LANES - MAX_KEY_BLOCKS, MXU_W), F32)
    for g4 in range(D // MXU_W):
        lo = g4 * MXU_W
        q_hi, q_lo = _split2(q_ref[:, lo:lo + MXU_W])
        km = jnp.concatenate([km_ref[:, lo:lo + MXU_W], zeros_km], axis=0)
        for hl in range(MXU_W // HEAD_DIM):
            head = 4 * g4 + hl
            km_hi, km_lo = _split2(jnp.where(km_lane_head == hl, km, 0.0))
            gate = _dot_nt(km_hi, q_hi) + _dot_nt(km_lo, q_hi) + _dot_nt(km_hi, q_lo)
            keep = blk == own
            gh = jnp.where(blk < MAX_KEY_BLOCKS, jnp.where(blk < own, gate, NEG_INF), -jnp.inf)
            for _ in range(MOBA_TOPK):
                mx = jnp.max(gh, axis=0, keepdims=True)
                idx = jnp.min(jnp.where(gh == mx, blk, LANES), axis=0, keepdims=True)
                pick = blk == idx
                keep = keep | (pick & (idx < own))
                gh = jnp.where(pick, -jnp.inf, gh)
            bias = jnp.where(keep, 0.0, NEG_INF).T
            qx = jnp.where(lane < MAX_KEY_BLOCKS, bias, slope_ref[head:head + 1, :])
            qx_ref[:, head * LANES:(head + 1) * LANES] = qx.astype(BF16)


def _moba_gate(qf, kmean, slope_lanes):
    s = qf.shape[0]
    n_kb = s // MOBA_BLOCK
    assert n_kb <= MAX_KEY_BLOCKS
    km = jnp.pad(kmean.reshape(n_kb, D), ((0, MAX_KEY_BLOCKS - n_kb), (0, 0)))
    return pl.pallas_call(
        _moba_gate_kernel,
        grid=(n_kb,),
        in_specs=[_rows(MOBA_BLOCK, D), _resident((MAX_KEY_BLOCKS, D)), _resident((N_HEADS, LANES))],
        out_specs=_rows(MOBA_BLOCK, N_HEADS * LANES),
        out_shape=jax.ShapeDtypeStruct((s, N_HEADS * LANES), BF16),
        compiler_params=_params("parallel"),
        name="moba_gate",
    )(qf, km, slope_lanes)


def _attn_kernel(*refs, t, moba):
    if moba:
        nvis_ref, r_ref, q_ref, kk_ref, kx_ref, v0_ref, v1_ref, qx_ref, o_ref, acc_ref, m_ref, s_ref = refs
    else:
        nvis_ref, r_ref, q_ref, kk_ref, kx_ref, v0_ref, v1_ref, o_ref, acc_ref, m_ref, s_ref = refs
    pair = pl.program_id(0)
    i = pl.program_id(1)
    q = q_ref[...]
    lane = lax.broadcasted_iota(jnp.int32, (t, LANES), 1)
    row = lax.broadcasted_iota(jnp.int32, (t, t), 0)
    col = lax.broadcasted_iota(jnp.int32, (t, t), 1)
    v_refs = (v0_ref, v1_ref)
    n_past = [nvis_ref[2 * pair + hh, i] for hh in range(2)]

    def query(hh):
        mine = (lane < HEAD_DIM) if hh == 0 else (lane >= HEAD_DIM)
        if moba:
            qx = qx_ref[:, hh * LANES:(hh + 1) * LANES]
        else:
            qx = jnp.where((lane >= 3 * hh) & (lane < 3 * hh + 3), 1.0, 0.0).astype(BF16)
        return jnp.concatenate([jnp.where(mine, q, jnp.zeros_like(q)), qx], axis=1)

    qh = [query(hh) for hh in range(2)]

    def scores(hh, j):
        start = pl.multiple_of(j * t, t)
        kk = jnp.concatenate([kk_ref[pl.ds(start, t), :], kx_ref[pl.ds(start, t), :]], axis=1)
        return _dot_nt(qh[hh], kk)

    def values(hh, j):
        return v_refs[hh][pl.ds(pl.multiple_of(j * t, t), t), :]

    def across(stat):
        return jnp.concatenate([stat] * (t // LANES), axis=1)

    def step(hh, j, cur, nxt):
        h = 2 * pair + hh
        s_ref[hh, nxt] = scores(hh, jnp.maximum(j - 1, 0))
        s = s_ref[hh, cur]
        if moba:
            dist = jnp.zeros((1, 1), jnp.int32) + (i - j) * t
            c = -(r_ref[h, 0] * dist.astype(F32))
        else:
            c = (jnp.zeros((1, 1), F32) + r_ref[h, i]) - (jnp.zeros((1, 1), F32) + r_ref[h, j])
        m_old = m_ref[hh]
        m_tile = jnp.broadcast_to(jnp.max(s, axis=1, keepdims=True), (t, LANES))
        m_new = jnp.maximum(m_old, m_tile + c)
        alpha = jnp.exp2(m_old - m_new)
        p = jnp.exp2(s - across(m_new - c))
        acc_ref[hh] = alpha * acc_ref[hh] + _dot(p.astype(BF16), values(hh, j))
        m_ref[hh] = m_new

    for hh in range(2):
        s = jnp.where(col <= row, scores(hh, i), NEG_INF)
        m = jnp.broadcast_to(jnp.max(s, axis=1, keepdims=True), (t, LANES))
        acc_ref[hh] = _dot(jnp.exp2(s - across(m)).astype(BF16), values(hh, i))
        m_ref[hh] = m
        s_ref[hh, 0] = scores(hh, jnp.maximum(i - 1, 0))

    n_joint = jnp.minimum(n_past[0], n_past[1]) // 2

    def joint_body(k2, carry):
        j = i - 1 - 2 * k2
        for hh in range(2):
            step(hh, j, 0, 1)
        for hh in range(2):
            step(hh, j - 1, 1, 0)
        return carry

    lax.fori_loop(0, n_joint, joint_body, 0)

    outs = []
    for hh in range(2):
        def tail_body(k2, carry, hh=hh):
            j = i - 1 - 2 * k2
            step(hh, j, 0, 1)

            @pl.when(2 * k2 + 1 < n_past[hh])
            def _():
                step(hh, j - 1, 1, 0)

            return carry

        lax.fori_loop(n_joint, (n_past[hh] + 1) // 2, tail_body, 0)
        acc = acc_ref[hh]
        outs.append(acc * (1.0 / pltpu.roll(acc, HEAD_DIM, axis=1)))
    o_ref[...] = jnp.where(lane < HEAD_DIM, outs[0], outs[1]).astype(BF16)


def _attention(nvis, r, q, kk, kx, v0, v1, qx=None, *, t):
    s = q.shape[0]
    moba = qx is not None
    col_block = pl.BlockSpec((s, LANES), lambda p, i, *_: (0, p), pipeline_mode=pl.Buffered(1))
    tile = pl.BlockSpec((t, LANES), lambda p, i, *_: (i, p))
    ins = [q, kk, kx, v0, v1]
    if moba:
        kx_spec = pl.BlockSpec((s, LANES), lambda p, i, *_: (0, 0), pipeline_mode=pl.Buffered(1))
        specs = [tile, col_block, kx_spec, col_block, col_block,
                 pl.BlockSpec((t, 2 * LANES), lambda p, i, *_: (i, p))]
        ins.append(qx)
    else:
        specs = [tile, col_block, col_block, col_block, col_block]
    return pl.pallas_call(
        functools.partial(_attn_kernel, t=t, moba=moba),
        grid_spec=pltpu.PrefetchScalarGridSpec(
            num_scalar_prefetch=2,
            grid=(N_PAIRS, s // t),
            in_specs=specs,
            out_specs=tile,
            scratch_shapes=[pltpu.VMEM((2, t, LANES), F32), pltpu.VMEM((2, t, LANES), F32),
                            pltpu.VMEM((2, 2, t, t), F32)],
        ),
        out_shape=jax.ShapeDtypeStruct((s, D), BF16),
        compiler_params=_params("parallel", "parallel"),
        name="moba_attn" if moba else "fox_attn",
    )(nvis, r, *ins)


def _first_visible(vis):
    nt = vis.shape[1]
    i = jnp.arange(nt)[None, :, None]
    j = jnp.arange(nt)[None, None, :]
    first = jnp.min(jnp.where(vis & (j < i), j, i), axis=2)
    return (i[:, :, 0] - first).astype(jnp.int32)


def _score_bound(qn, kn):
    return 1.02 * HEAD_DIM ** 0.5 * jnp.max(jnp.abs(qn)) * jnp.max(jnp.abs(kn))


def _fox_schedule(cum_f, qn, kn, t):
    f = cum_f[:, :N_HEADS].T
    f_before = jnp.concatenate([jnp.zeros((N_HEADS, 1), F32), f[:, t - 1:-1:t]], axis=1)
    reach = SKIP_THRESHOLD + 2.0 * _score_bound(qn, kn)
    vis = (f[:, 0::t, None] - f[:, None, t - 1::t]) >= -reach
    return _first_visible(vis), LOG2E * f_before


def _moba_constants(s, qn, kn, t):
    slopes = jnp.exp2(-ALIBI_MAX_EXP * jnp.arange(1, N_HEADS + 1, dtype=F32) / N_HEADS)
    hi, mid, lo = _split3(slopes * LOG2E)
    pieces = jnp.stack([hi, mid, lo, hi, mid, lo], axis=1).astype(F32)
    slope_lanes = jnp.zeros((N_HEADS, LANES), F32).at[:, MAX_KEY_BLOCKS:MAX_KEY_BLOCKS + 6].set(pieces)
    nt = s // t
    i = jnp.arange(nt)[None, :, None]
    j = jnp.arange(nt)[None, None, :]
    nearest = ((i - j - 1) * t + 1).astype(F32)
    reach = SKIP_THRESHOLD + 2.0 * _score_bound(qn, kn)
    nvis = _first_visible(slopes[:, None, None] * nearest <= reach)
    r = jnp.broadcast_to((slopes * LOG2E)[:, None], (N_HEADS, nt))
    return slope_lanes, nvis, r


def kernel(x, c, ln_mix_g, ln_ffn_g, ada_w, ada_b, ffn_up, ffn_conv, ffn_down, sc_in, sc_conv, sc_out, cf_in, cf_dw, cf_dw_b, cf_ln_g, cf_ln_b, cf_out, fox_in, fox_bf, fox_qn, fox_kn, fox_out, moba_in, moba_qn, moba_kn, moba_out):
    batch, s, _ = x.shape
    assert batch == 1 and c.shape[0] == 1
    depth = ada_w.shape[0]
    mods = _ada(c, ada_w, ada_b)
    xs = x[0]
    for i in range(depth):
        kind, j = i % 4, i // 4
        mod = mods[i]
        pre = None
        if kind == 0:
            xs = _sc_mixer(xs, mod, ln_mix_g[i], sc_in[j].astype(BF16), sc_conv[j],
                           sc_out[j].astype(BF16))
        elif kind == 1:
            xs = _cf_mixer(xs, mod, ln_mix_g[i], cf_in[j].astype(BF16), cf_dw[j], cf_dw_b[j],
                           cf_ln_g[j], cf_ln_b[j], cf_out[j].astype(BF16))
        elif kind == 2:
            q, k, v0, v1, kx, cum_f = _fox_proj(xs, mod, ln_mix_g[i], fox_in[j], fox_bf[j],
                                                fox_qn[j], fox_kn[j])
            nvis, r = _fox_schedule(cum_f, fox_qn[j], fox_kn[j], ATTN_TILE)
            pre = (_attention(nvis, r, q, k, kx, v0, v1, t=ATTN_TILE), fox_out[j].astype(BF16))
        else:
            q, k, v0, v1, qf, kmean, kx = _moba_proj(xs, mod, ln_mix_g[i], moba_in[j].astype(BF16),
                                                     moba_qn[j], moba_kn[j])
            slope_lanes, nvis, r = _moba_constants(s, moba_qn[j], moba_kn[j], ATTN_TILE)
            qx = _moba_gate(qf, kmean, slope_lanes)
            pre = (_attention(nvis, r, q, k, kx, v0, v1, qx, t=ATTN_TILE), moba_out[j].astype(BF16))
        xs = _ffn(xs, mod, ln_ffn_g[i], ffn_up[i].astype(BF16), ffn_conv[i],
                  ffn_down[i].astype(BF16), pre)
    return xs[None]
```

```python
import functools

import jax
import jax.numpy as jnp
from jax import lax
from jax.experimental import pallas as pl
from jax.experimental.pallas import tpu as pltpu

D = 1024
N_HEADS = 16
HEAD_DIM = D // N_HEADS
N_PAIRS = N_HEADS // 2
D_FF = ((8 * D // 3 + 127) // 128) * 128
CONFORMER_K = 31
MOBA_BLOCK = 256
MOBA_TOPK = 3
ALIBI_MAX_EXP = 8.0
RMS_EPS = 1e-6
LN_EPS = 1e-5
NEG_INF = -1e30
SCALE = HEAD_DIM ** -0.5
LOG2E = 1.4426950408889634
SKIP_THRESHOLD = 106.0
ATTN_TILE = 512
MAX_KEY_BLOCKS = 64

LANES = 128
SUBLANES = 8
MXU_W = 256
VMEM_LIMIT = 56 * 1024 * 1024

F32 = jnp.float32
BF16 = jnp.bfloat16

_dot = functools.partial(jnp.dot, preferred_element_type=F32)


def _dot_nt(a, b):
    return lax.dot_general(a, b, (((1,), (1,)), ((), ())), preferred_element_type=F32)


def _resident(shape):
    nd = len(shape)
    return pl.BlockSpec(shape, lambda *_: (0,) * nd, pipeline_mode=pl.Buffered(1))


def _rows(tm, width):
    return pl.BlockSpec((tm, width), lambda i: (i, 0))


def _params(*sem):
    return pltpu.CompilerParams(dimension_semantics=sem, vmem_limit_bytes=VMEM_LIMIT)


def _rms_mod(x, g, scale, shift):
    ms = jnp.mean(x * x, axis=-1, keepdims=True)
    return (x * lax.rsqrt(ms + RMS_EPS) * g) * (1.0 + scale) + shift


def _shifted(ext, s, halo, tm):
    a, b = divmod(s, SUBLANES)
    r = pltpu.roll(ext, b, axis=0) if b else ext
    lo = halo - a * SUBLANES
    return r[lo:lo + tm, :]


def _split2(z):
    hi = z.astype(BF16)
    lo = (z - hi.astype(F32)).astype(BF16)
    return hi, lo


def _split3(z):
    hi = z.astype(BF16)
    r = z - hi.astype(F32)
    mid = r.astype(BF16)
    lo = (r - mid.astype(F32)).astype(BF16)
    return hi, mid, lo


def _ada_kernel(c_ref, w_ref, b_ref, o_ref):
    cond = jax.nn.silu(c_ref[...])
    o_ref[0, 0] = jnp.sum(w_ref[0] * cond, axis=0, keepdims=True) + b_ref[0, 0]


def _ada(c, ada_w, ada_b):
    depth = ada_w.shape[0]
    out = pl.pallas_call(
        _ada_kernel,
        grid=(depth, 6),
        in_specs=[
            pl.BlockSpec((D, 1), lambda l, j: (0, 0)),
            pl.BlockSpec((1, D, D), lambda l, j: (l, 0, j)),
            pl.BlockSpec((1, 1, 1, D), lambda l, j: (l, j, 0, 0)),
        ],
        out_specs=pl.BlockSpec((1, 1, 1, D), lambda l, j: (l, j, 0, 0)),
        out_shape=jax.ShapeDtypeStruct((depth, 6, 1, D), F32),
        compiler_params=_params("parallel", "parallel"),
        name="ada",
    )(c.reshape(D, 1), ada_w, ada_b.reshape(depth, 6, 1, D))
    return out.reshape(depth, 6, D)


def _ffn_kernel(*refs, tm, cw, pre_proj):
    if pre_proj:
        x_ref, a_ref, wo_ref, mod_ref, g_ref, wup_ref, wc_ref, wdn_ref, o_ref, carry_ref = refs
    else:
        x_ref, mod_ref, g_ref, wup_ref, wc_ref, wdn_ref, o_ref, carry_ref = refs

    @pl.when(pl.program_id(0) == 0)
    def _():
        carry_ref[...] = jnp.zeros_like(carry_ref)

    x = x_ref[...]
    if pre_proj:
        x = x + mod_ref[2:3, :] * _dot(a_ref[...], wo_ref[...])
    h = _rms_mod(x, g_ref[...], mod_ref[4:5, :], mod_ref[3:4, :]).astype(BF16)
    acc = jnp.zeros((tm, D), F32)
    for c in range(D_FF // cw):
        lo = c * cw
        u = _dot(h, wup_ref[:, lo:lo + cw])
        gate = _dot(h, wup_ref[:, D_FF + lo:D_FF + lo + cw])
        ext = jnp.concatenate([carry_ref[:, lo:lo + cw], u], axis=0)
        carry_ref[:, lo:lo + cw] = u[tm - SUBLANES:, :]
        conv = (wc_ref[0:1, lo:lo + cw] * _shifted(ext, 2, SUBLANES, tm)
                + wc_ref[1:2, lo:lo + cw] * _shifted(ext, 1, SUBLANES, tm)
                + wc_ref[2:3, lo:lo + cw] * u)
        act = (conv * jax.nn.sigmoid(conv)) * gate
        acc = acc + _dot(act.astype(BF16), wdn_ref[lo:lo + cw, :])
    o_ref[...] = x + mod_ref[5:6, :] * acc


def _ffn(x, mod, g, w_up, w_conv, w_down, pre=None, *, tm=512, cw=256):
    s = x.shape[0]
    ins, specs = [x], [_rows(tm, D)]
    if pre is not None:
        a, w_o = pre
        ins += [a, w_o]
        specs += [_rows(tm, D), _resident((D, D))]
    ins += [mod, g.reshape(1, D), w_up, w_conv, w_down]
    specs += [_resident((6, D)), _resident((1, D)), _resident((D, 2 * D_FF)),
              _resident((3, D_FF)), _resident((D_FF, D))]
    return pl.pallas_call(
        functools.partial(_ffn_kernel, tm=tm, cw=cw, pre_proj=pre is not None),
        grid=(s // tm,),
        in_specs=specs,
        out_specs=_rows(tm, D),
        out_shape=jax.ShapeDtypeStruct((s, D), F32),
        scratch_shapes=[pltpu.VMEM((SUBLANES, D_FF), F32)],
        compiler_params=_params("arbitrary"),
        name="ffn_proj" if pre is not None else "ffn",
    )(*ins)


def _sc_kernel(x_ref, mod_ref, g_ref, win_ref, wc_ref, wout_ref, o_ref, carry_ref, *, tm, cw):
    @pl.when(pl.program_id(0) == 0)
    def _():
        carry_ref[...] = jnp.zeros_like(carry_ref)

    x = x_ref[...]
    h = _rms_mod(x, g_ref[...], mod_ref[1:2, :], mod_ref[0:1, :]).astype(BF16)
    acc = jnp.zeros((tm, D), F32)
    for c in range(D // cw):
        lo = c * cw
        b_gate = _dot(h, win_ref[:, lo:lo + cw])
        c_gate = _dot(h, win_ref[:, D + lo:D + lo + cw])
        xh = _dot(h, win_ref[:, 2 * D + lo:2 * D + lo + cw])
        u = c_gate * xh
        ext = jnp.concatenate([carry_ref[:, lo:lo + cw], u], axis=0)
        carry_ref[:, lo:lo + cw] = u[tm - SUBLANES:, :]
        conv = (wc_ref[0:1, lo:lo + cw] * _shifted(ext, 2, SUBLANES, tm)
                + wc_ref[1:2, lo:lo + cw] * _shifted(ext, 1, SUBLANES, tm)
                + wc_ref[2:3, lo:lo + cw] * u)
        acc = acc + _dot((b_gate * conv).astype(BF16), wout_ref[lo:lo + cw, :])
    o_ref[...] = x + mod_ref[2:3, :] * acc


def _sc_mixer(x, mod, g, w_in, w_conv, w_out, *, tm=512, cw=256):
    s = x.shape[0]
    return pl.pallas_call(
        functools.partial(_sc_kernel, tm=tm, cw=cw),
        grid=(s // tm,),
        in_specs=[_rows(tm, D), _resident((6, D)), _resident((1, D)), _resident((D, 3 * D)),
                  _resident((3, D)), _resident((D, D))],
        out_specs=_rows(tm, D),
        out_shape=jax.ShapeDtypeStruct((s, D), F32),
        scratch_shapes=[pltpu.VMEM((SUBLANES, D), F32)],
        compiler_params=_params("arbitrary"),
        name="sc_mixer",
    )(x, mod, g.reshape(1, D), w_in, w_conv, w_out)


CF_HALO = 32


def _cf_kernel(x_ref, mod_ref, g_ref, win_ref, wdw_ref, bdw_ref, lng_ref, lnb_ref, wout_ref,
               o_ref, carry_ref, u_ref, *, tm, cw):
    @pl.when(pl.program_id(0) == 0)
    def _():
        carry_ref[...] = jnp.zeros_like(carry_ref)

    x = x_ref[...]
    h = _rms_mod(x, g_ref[...], mod_ref[1:2, :], mod_ref[0:1, :]).astype(BF16)
    for c in range(D // cw):
        lo = c * cw
        a = _dot(h, win_ref[:, lo:lo + cw])
        gate = _dot(h, win_ref[:, D + lo:D + lo + cw])
        glu = a * jax.nn.sigmoid(gate)
        ext = jnp.concatenate([carry_ref[:, lo:lo + cw], glu], axis=0)
        carry_ref[:, lo:lo + cw] = glu[tm - CF_HALO:, :]
        conv = jnp.zeros((tm, cw), F32) + bdw_ref[:, lo:lo + cw]
        for b in range(SUBLANES):
            r = pltpu.roll(ext, b, axis=0) if b else ext
            for a8 in range(CF_HALO // SUBLANES):
                s = a8 * SUBLANES + b
                if s < CONFORMER_K:
                    k = CONFORMER_K - 1 - s
                    start = CF_HALO - a8 * SUBLANES
                    conv = conv + wdw_ref[k:k + 1, lo:lo + cw] * r[start:start + tm, :]
        u_ref[:, lo:lo + cw] = conv
    u = u_ref[...]
    uc = u - jnp.mean(u, axis=-1, keepdims=True)
    y = uc * lax.rsqrt(jnp.mean(uc * uc, axis=-1, keepdims=True) + LN_EPS)
    y = y * lng_ref[...] + lnb_ref[...]
    y = (y * jax.nn.sigmoid(y)).astype(BF16)
    o_ref[...] = x + mod_ref[2:3, :] * _dot(y, wout_ref[...])


def _cf_mixer(x, mod, g, w_in, w_dw, b_dw, ln_g, ln_b, w_out, *, tm=512, cw=256):
    s = x.shape[0]
    return pl.pallas_call(
        functools.partial(_cf_kernel, tm=tm, cw=cw),
        grid=(s // tm,),
        in_specs=[_rows(tm, D), _resident((6, D)), _resident((1, D)), _resident((D, 2 * D)),
                  _resident((CONFORMER_K, D)), _resident((1, D)), _resident((1, D)),
                  _resident((1, D)), _resident((D, D))],
        out_specs=_rows(tm, D),
        out_shape=jax.ShapeDtypeStruct((s, D), F32),
        scratch_shapes=[pltpu.VMEM((CF_HALO, D), F32), pltpu.VMEM((tm, D), F32)],
        compiler_params=_params("arbitrary"),
        name="cf_mixer",
    )(x, mod, g.reshape(1, D), w_in, w_dw, b_dw.reshape(1, D), ln_g.reshape(1, D),
      ln_b.reshape(1, D), w_out)


def _group_mean_matrix(n):
    r = lax.broadcasted_iota(jnp.int32, (n, n), 0) // HEAD_DIM
    c = lax.broadcasted_iota(jnp.int32, (n, n), 1) // HEAD_DIM
    return jnp.where(r == c, 1.0 / HEAD_DIM, 0.0).astype(BF16)


def _head_rms(z, gm):
    hi, lo = _split2(z * z)
    ms = _dot(hi, gm) + _dot(lo, gm)
    return z * lax.rsqrt(ms + RMS_EPS)


def _store_v_with_ones(v, v0_ref, v1_ref, lo, cw):
    lane = lax.broadcasted_iota(jnp.int32, v.shape, 1)
    first = (lane & (LANES - 1)) < HEAD_DIM
    v0_ref[:, lo:lo + cw] = jnp.where(first, v, 1.0).astype(BF16)
    v1_ref[:, lo:lo + cw] = jnp.where(first, 1.0, v).astype(BF16)


def _fox_proj_kernel(x_ref, mod_ref, g_ref, w_ref, bf_ref, qn_ref, kn_ref,
                     q_ref, k_ref, v0_ref, v1_ref, kx_ref, f_ref, carry_ref, tri_ref, *, tm, cw):
    @pl.when(pl.program_id(0) == 0)
    def _():
        carry_ref[...] = jnp.zeros_like(carry_ref)
        r = lax.broadcasted_iota(jnp.int32, (tm, tm), 0)
        c = lax.broadcasted_iota(jnp.int32, (tm, tm), 1)
        tri_ref[...] = jnp.where(r >= c, 1.0, 0.0).astype(BF16)

    h = _rms_mod(x_ref[...], g_ref[...], mod_ref[1:2, :], mod_ref[0:1, :]).astype(BF16)
    gm = _group_mean_matrix(cw)
    for c in range(D // cw):
        lo = c * cw
        q = _head_rms(_dot(h, w_ref[:, lo:lo + cw]), gm)
        q_ref[:, lo:lo + cw] = (q * qn_ref[:, lo:lo + cw] * (SCALE * LOG2E)).astype(BF16)
        k = _head_rms(_dot(h, w_ref[:, D + lo:D + lo + cw]), gm)
        k_ref[:, lo:lo + cw] = (k * kn_ref[:, lo:lo + cw]).astype(BF16)
        _store_v_with_ones(_dot(h, w_ref[:, 2 * D + lo:2 * D + lo + cw]), v0_ref, v1_ref, lo, cw)
    fl = _dot(h, w_ref[:, 3 * D:3 * D + LANES]) + bf_ref[...]
    lf = jnp.minimum(fl, 0.0) - jnp.log1p(jnp.exp(-jnp.abs(fl)))
    tri = tri_ref[...]
    hi, mid, lo3 = _split3(lf)
    local = _dot(tri, hi) + _dot(tri, mid) + _dot(tri, lo3)
    cum = local + carry_ref[0:1, :]
    f_ref[...] = cum
    carry_ref[...] = jnp.broadcast_to(cum[tm - 1:tm, :], carry_ref.shape)
    head = lax.broadcasted_iota(jnp.int32, (LANES, D), 0)
    lane = lax.broadcasted_iota(jnp.int32, (LANES, D), 1)
    target = lax.shift_right_logical(head, 1) * LANES + (head & 1) * 3
    kx = jnp.zeros((tm, D), F32)
    for t, piece in enumerate(_split3(local * (-LOG2E))):
        place = jnp.where((lane == target + t) & (head < N_HEADS), 1.0, 0.0).astype(BF16)
        kx = kx + _dot(piece, place)
    kx_ref[...] = kx.astype(BF16)


def _fox_proj(x, mod, g, w_in, b_f, qn, kn, *, tm=ATTN_TILE, cw=256):
    s = x.shape[0]
    w = jnp.pad(w_in, ((0, 0), (0, LANES - N_HEADS))).astype(BF16)
    bf = jnp.pad(b_f, (0, LANES - N_HEADS)).reshape(1, LANES)
    return pl.pallas_call(
        functools.partial(_fox_proj_kernel, tm=tm, cw=cw),
        grid=(s // tm,),
        in_specs=[_rows(tm, D), _resident((6, D)), _resident((1, D)),
                  _resident((D, 3 * D + LANES)), _resident((1, LANES)),
                  _resident((1, D)), _resident((1, D))],
        out_specs=[_rows(tm, D)] * 5 + [_rows(tm, LANES)],
        out_shape=[jax.ShapeDtypeStruct((s, D), BF16)] * 5 + [jax.ShapeDtypeStruct((s, LANES), F32)],
        scratch_shapes=[pltpu.VMEM((SUBLANES, LANES), F32), pltpu.VMEM((tm, tm), BF16)],
        compiler_params=_params("arbitrary"),
        name="fox_proj",
    )(x, mod, g.reshape(1, D), w, bf, jnp.tile(qn, N_HEADS).reshape(1, D),
      jnp.tile(kn, N_HEADS).reshape(1, D))


def _moba_proj_kernel(x_ref, mod_ref, g_ref, w_ref, qn_ref, kn_ref,
                      q_ref, k_ref, v0_ref, v1_ref, qf_ref, km_ref, kx_ref, *, tm, cw):
    row = lax.broadcasted_iota(jnp.int32, (tm, LANES), 0)
    lane = lax.broadcasted_iota(jnp.int32, (tm, LANES), 1)
    rem = row & (MOBA_BLOCK - 1)
    block = pl.program_id(0) * (tm // MOBA_BLOCK) + lax.shift_right_logical(row, MOBA_BLOCK.bit_length() - 1)
    offset_lanes = jnp.where(lane < MAX_KEY_BLOCKS + 3, rem, row - rem).astype(F32)
    kx_ref[...] = jnp.where(
        lane < MAX_KEY_BLOCKS, jnp.where(lane == block, 1.0, 0.0),
        jnp.where(lane < MAX_KEY_BLOCKS + 6, offset_lanes, 0.0)).astype(BF16)

    h = _rms_mod(x_ref[...], g_ref[...], mod_ref[1:2, :], mod_ref[0:1, :]).astype(BF16)
    gm = _group_mean_matrix(cw)
    for c in range(D // cw):
        lo = c * cw
        q = _head_rms(_dot(h, w_ref[:, lo:lo + cw]), gm) * qn_ref[:, lo:lo + cw]
        qf_ref[:, lo:lo + cw] = q
        q_ref[:, lo:lo + cw] = (q * (SCALE * LOG2E)).astype(BF16)
        k = _head_rms(_dot(h, w_ref[:, D + lo:D + lo + cw]), gm) * kn_ref[:, lo:lo + cw]
        k_ref[:, lo:lo + cw] = k.astype(BF16)
        for b in range(tm // MOBA_BLOCK):
            km_ref[b, :, lo:lo + cw] = jnp.mean(
                k[b * MOBA_BLOCK:(b + 1) * MOBA_BLOCK, :], axis=0, keepdims=True)
        _store_v_with_ones(_dot(h, w_ref[:, 2 * D + lo:2 * D + lo + cw]), v0_ref, v1_ref, lo, cw)


def _moba_proj(x, mod, g, w_in, qn, kn, *, tm=ATTN_TILE, cw=256):
    s = x.shape[0]
    nb = tm // MOBA_BLOCK
    return pl.pallas_call(
        functools.partial(_moba_proj_kernel, tm=tm, cw=cw),
        grid=(s // tm,),
        in_specs=[_rows(tm, D), _resident((6, D)), _resident((1, D)), _resident((D, 3 * D)),
                  _resident((1, D)), _resident((1, D))],
        out_specs=[_rows(tm, D)] * 5 + [pl.BlockSpec((nb, 1, D), lambda i: (i, 0, 0)),
                                        _rows(tm, LANES)],
        out_shape=[jax.ShapeDtypeStruct((s, D), BF16)] * 4
        + [jax.ShapeDtypeStruct((s, D), F32), jax.ShapeDtypeStruct((s // MOBA_BLOCK, 1, D), F32),
           jax.ShapeDtypeStruct((s, LANES), BF16)],
        compiler_params=_params("parallel"),
        name="moba_proj",
    )(x, mod, g.reshape(1, D), w_in, jnp.tile(qn, N_HEADS).reshape(1, D),
      jnp.tile(kn, N_HEADS).reshape(1, D))


def _moba_gate_kernel(q_ref, km_ref, slope_ref, qx_ref):
    own = pl.program_id(0)
    t = q_ref.shape[0]
    blk = lax.broadcasted_iota(jnp.int32, (MAX_KEY_BLOCKS, t), 0)
    lane = lax.broadcasted_iota(jnp.int32, (t, LANES), 1)
    km_lane_head = lax.shift_right_logical(
        lax.broadcasted_iota(jnp.int32, (MAX_KEY_BLOCKS, MXU_W), 1), HEAD_DIM.bit_length() - 1)
    pad_rows = jnp.full((LANES - MAX_KEY_BLOCKS, t), NEG_INF, F32)
    for g4 in range(D // MXU_W):
        lo = g4 * MXU_W
        q_hi, q_lo = _split2(q_ref[:, lo:lo + MXU_W])
        km = km_ref[:, lo:lo + MXU_W]
        for hl in range(MXU_W // HEAD_DIM):
            head = 4 * g4 + hl
            km_hi, km_lo = _split2(jnp.where(km_lane_head == hl, km, 0.0))
            gate = _dot_nt(km_hi, q_hi) + _dot_nt(km_lo, q_hi) + _dot_nt(km_hi, q_lo)
            keep = blk == own
            gh = jnp.where(blk < own, gate, NEG_INF)
            for _ in range(MOBA_TOPK):
                mx = jnp.max(gh, axis=0, keepdims=True)
                idx = jnp.min(jnp.where(gh == mx, blk, MAX_KEY_BLOCKS), axis=0, keepdims=True)
                pick = blk == idx
                keep = keep | (pick & (idx < own))
                gh = jnp.where(pick, -jnp.inf, gh)
            bias = jnp.concatenate([jnp.where(keep, 0.0, NEG_INF), pad_rows], axis=0).T
            qx = jnp.where(lane < MAX_KEY_BLOCKS, bias, slope_ref[head:head + 1, :])
            qx_ref[:, head * LANES:(head + 1) * LANES] = qx.astype(BF16)


def _moba_gate(qf, kmean, slope_lanes):
    s = qf.shape[0]
    n_kb = s // MOBA_BLOCK
    assert n_kb <= MAX_KEY_BLOCKS
    km = jnp.pad(kmean.reshape(n_kb, D), ((0, MAX_KEY_BLOCKS - n_kb), (0, 0)))
    return pl.pallas_call(
        _moba_gate_kernel,
        grid=(n_kb,),
        in_specs=[_rows(MOBA_BLOCK, D), _resident((MAX_KEY_BLOCKS, D)), _resident((N_HEADS, LANES))],
        out_specs=_rows(MOBA_BLOCK, N_HEADS * LANES),
        out_shape=jax.ShapeDtypeStruct((s, N_HEADS * LANES), BF16),
        compiler_params=_params("parallel"),
        name="moba_gate",
    )(qf, km, slope_lanes)


def _attn_kernel(*refs, t, moba):
    if moba:
        nvis_ref, r_ref, q_ref, kk_ref, kx_ref, v0_ref, v1_ref, qx_ref, o_ref, acc_ref, m_ref, s_ref, mp_ref = refs
    else:
        nvis_ref, r_ref, q_ref, kk_ref, kx_ref, v0_ref, v1_ref, o_ref, acc_ref, m_ref, s_ref, mp_ref = refs
    pair = pl.program_id(0)
    i = pl.program_id(1)
    q = q_ref[...]
    lane = lax.broadcasted_iota(jnp.int32, (t, LANES), 1)
    row = lax.broadcasted_iota(jnp.int32, (t, t), 0)
    col = lax.broadcasted_iota(jnp.int32, (t, t), 1)
    v_refs = (v0_ref, v1_ref)
    n_past = [nvis_ref[2 * pair + hh, i] for hh in range(2)]

    def query(hh):
        mine = (lane < HEAD_DIM) if hh == 0 else (lane >= HEAD_DIM)
        if moba:
            qx = qx_ref[:, hh * LANES:(hh + 1) * LANES]
        else:
            qx = jnp.where((lane >= 3 * hh) & (lane < 3 * hh + 3), 1.0, 0.0).astype(BF16)
        return jnp.concatenate([jnp.where(mine, q, jnp.zeros_like(q)), qx], axis=1)

    qh = [query(hh) for hh in range(2)]

    def scores(hh, j):
        start = pl.multiple_of(j * t, t)
        kk = jnp.concatenate([kk_ref[pl.ds(start, t), :], kx_ref[pl.ds(start, t), :]], axis=1)
        return _dot_nt(qh[hh], kk)

    def values(hh, j):
        return v_refs[hh][pl.ds(pl.multiple_of(j * t, t), t), :]

    def across(stat):
        return jnp.concatenate([stat] * (t // LANES), axis=1)

    def produce(hh, j, slot):
        s = scores(hh, j)
        s_ref[hh, slot] = s
        part = s[:, 0:LANES]
        for g in range(1, t // LANES):
            part = jnp.maximum(part, s[:, g * LANES:(g + 1) * LANES])
        mp_ref[hh, slot] = part

    def step(hh, j, cur, nxt=None):
        h = 2 * pair + hh
        if nxt is not None:
            produce(hh, jnp.maximum(j - 1, 0), nxt)
        s = s_ref[hh, cur]
        if moba:
            dist = jnp.zeros((1, 1), jnp.int32) + (i - j) * t
            c = -(r_ref[h, 0] * dist.astype(F32))
        else:
            c = (jnp.zeros((1, 1), F32) + r_ref[h, i]) - (jnp.zeros((1, 1), F32) + r_ref[h, j])
        m_old = m_ref[hh]
        m_tile = jnp.broadcast_to(jnp.max(mp_ref[hh, cur], axis=1, keepdims=True), (t, LANES))
        m_new = jnp.maximum(m_old, m_tile + c)
        alpha = jnp.exp2(m_old - m_new)
        p = jnp.exp2(s - across(m_new - c))
        acc_ref[hh] = alpha * acc_ref[hh] + _dot(p.astype(BF16), values(hh, j))
        m_ref[hh] = m_new

    for hh in range(2):
        s = jnp.where(col <= row, scores(hh, i), NEG_INF)
        m = jnp.broadcast_to(jnp.max(s, axis=1, keepdims=True), (t, LANES))
        acc_ref[hh] = _dot(jnp.exp2(s - across(m)).astype(BF16), values(hh, i))
        m_ref[hh] = m
        produce(hh, jnp.maximum(i - 1, 0), 0)

    n_joint = jnp.minimum(n_past[0], n_past[1]) // 2

    def joint_body(k2, carry):
        j = i - 1 - 2 * k2
        for hh in range(2):
            step(hh, j, 0, 1)
        for hh in range(2):
            step(hh, j - 1, 1, 0)
        return carry

    lax.fori_loop(0, n_joint, joint_body, 0)

    outs = []
    for hh in range(2):
        def tail_body(k2, carry, hh=hh):
            j = i - 1 - 2 * k2
            step(hh, j, 0, 1)
            step(hh, j - 1, 1, 0)
            return carry

        n_pairs = n_past[hh] // 2
        lax.fori_loop(n_joint, n_pairs, tail_body, 0)

        @pl.when(n_past[hh] % 2 == 1)
        def _(hh=hh, n_pairs=n_pairs):
            step(hh, i - 1 - 2 * n_pairs, 0)

        acc = acc_ref[hh]
        outs.append(acc * (1.0 / pltpu.roll(acc, HEAD_DIM, axis=1)))
    o_ref[...] = jnp.where(lane < HEAD_DIM, outs[0], outs[1]).astype(BF16)


def _attention(nvis, r, q, kk, kx, v0, v1, qx=None, *, t):
    s = q.shape[0]
    moba = qx is not None
    col_block = pl.BlockSpec((s, LANES), lambda p, i, *_: (0, p), pipeline_mode=pl.Buffered(1))
    tile = pl.BlockSpec((t, LANES), lambda p, i, *_: (i, p))
    ins = [q, kk, kx, v0, v1]
    if moba:
        kx_spec = pl.BlockSpec((s, LANES), lambda p, i, *_: (0, 0), pipeline_mode=pl.Buffered(1))
        specs = [tile, col_block, kx_spec, col_block, col_block,
                 pl.BlockSpec((t, 2 * LANES), lambda p, i, *_: (i, p))]
        ins.append(qx)
    else:
        specs = [tile, col_block, col_block, col_block, col_block]
    return pl.pallas_call(
        functools.partial(_attn_kernel, t=t, moba=moba),
        grid_spec=pltpu.PrefetchScalarGridSpec(
            num_scalar_prefetch=2,
            grid=(N_PAIRS, s // t),
            in_specs=specs,
            out_specs=tile,
            scratch_shapes=[pltpu.VMEM((2, t, LANES), F32), pltpu.VMEM((2, t, LANES), F32),
                            pltpu.VMEM((2, 2, t, t), F32), pltpu.VMEM((2, 2, t, LANES), F32)],
        ),
        out_shape=jax.ShapeDtypeStruct((s, D), BF16),
        compiler_params=_params("parallel", "parallel"),
        name="moba_attn" if moba else "fox_attn",
    )(nvis, r, *ins)


def _first_visible(vis):
    nt = vis.shape[1]
    i = jnp.arange(nt)[None, :, None]
    j = jnp.arange(nt)[None, None, :]
    first = jnp.min(jnp.where(vis & (j < i), j, i), axis=2)
    return (i[:, :, 0] - first).astype(jnp.int32)


def _score_bound(qn, kn):
    return 1.02 * HEAD_DIM ** 0.5 * jnp.max(jnp.abs(qn)) * jnp.max(jnp.abs(kn))


def _fox_schedule(cum_f, qn, kn, t):
    f = cum_f[:, :N_HEADS].T
    f_before = jnp.concatenate([jnp.zeros((N_HEADS, 1), F32), f[:, t - 1:-1:t]], axis=1)
    reach = SKIP_THRESHOLD + 2.0 * _score_bound(qn, kn)
    vis = (f[:, 0::t, None] - f[:, None, t - 1::t]) >= -reach
    return _first_visible(vis), LOG2E * f_before


def _moba_constants(s, qn, kn, t):
    slopes = jnp.exp2(-ALIBI_MAX_EXP * jnp.arange(1, N_HEADS + 1, dtype=F32) / N_HEADS)
    hi, mid, lo = _split3(slopes * LOG2E)
    pieces = jnp.stack([hi, mid, lo, hi, mid, lo], axis=1).astype(F32)
    slope_lanes = jnp.zeros((N_HEADS, LANES), F32).at[:, MAX_KEY_BLOCKS:MAX_KEY_BLOCKS + 6].set(pieces)
    nt = s // t
    i = jnp.arange(nt)[None, :, None]
    j = jnp.arange(nt)[None, None, :]
    nearest = ((i - j - 1) * t + 1).astype(F32)
    reach = SKIP_THRESHOLD + 2.0 * _score_bound(qn, kn)
    nvis = _first_visible(slopes[:, None, None] * nearest <= reach)
    r = jnp.broadcast_to((slopes * LOG2E)[:, None], (N_HEADS, nt))
    return slope_lanes, nvis, r


def kernel(x, c, ln_mix_g, ln_ffn_g, ada_w, ada_b, ffn_up, ffn_conv, ffn_down, sc_in, sc_conv, sc_out, cf_in, cf_dw, cf_dw_b, cf_ln_g, cf_ln_b, cf_out, fox_in, fox_bf, fox_qn, fox_kn, fox_out, moba_in, moba_qn, moba_kn, moba_out):
    batch, s, _ = x.shape
    assert batch == 1 and c.shape[0] == 1
    depth = ada_w.shape[0]
    mods = _ada(c, ada_w, ada_b)
    xs = x.reshape(s, D)
    for i in range(depth):
        kind, j = i % 4, i // 4
        mod = mods[i]
        pre = None
        if kind == 0:
            xs = _sc_mixer(xs, mod, ln_mix_g[i], sc_in[j].astype(BF16), sc_conv[j],
                           sc_out[j].astype(BF16))
        elif kind == 1:
            xs = _cf_mixer(xs, mod, ln_mix_g[i], cf_in[j].astype(BF16), cf_dw[j], cf_dw_b[j],
                           cf_ln_g[j], cf_ln_b[j], cf_out[j].astype(BF16))
        elif kind == 2:
            q, k, v0, v1, kx, cum_f = _fox_proj(xs, mod, ln_mix_g[i], fox_in[j], fox_bf[j],
                                                fox_qn[j], fox_kn[j])
            nvis, r = _fox_schedule(cum_f, fox_qn[j], fox_kn[j], ATTN_TILE)
            pre = (_attention(nvis, r, q, k, kx, v0, v1, t=ATTN_TILE), fox_out[j].astype(BF16))
        else:
            q, k, v0, v1, qf, kmean, kx = _moba_proj(xs, mod, ln_mix_g[i], moba_in[j].astype(BF16),
                                                     moba_qn[j], moba_kn[j])
            slope_lanes, nvis, r = _moba_constants(s, moba_qn[j], moba_kn[j], ATTN_TILE)
            qx = _moba_gate(qf, kmean, slope_lanes)
            pre = (_attention(nvis, r, q, k, kx, v0, v1, qx, t=ATTN_TILE), moba_out[j].astype(BF16))
        xs = _ffn(xs, mod, ln_ffn_g[i], ffn_up[i].astype(BF16), ffn_conv[i],
                  ffn_down[i].astype(BF16), pre)
    return xs.reshape(1, s, D)
```

```python
import functools

import jax
import jax.numpy as jnp
from jax import lax
from jax.experimental import pallas as pl
from jax.experimental.pallas import tpu as pltpu

D = 1024
N_HEADS = 16
HEAD_DIM = D // N_HEADS
N_PAIRS = N_HEADS // 2
D_FF = ((8 * D // 3 + 127) // 128) * 128
CONFORMER_K = 31
MOBA_BLOCK = 256
MOBA_TOPK = 3
ALIBI_MAX_EXP = 8.0
RMS_EPS = 1e-6
LN_EPS = 1e-5
NEG_INF = -1e30
SCALE = HEAD_DIM ** -0.5
LOG2E = 1.4426950408889634
SKIP_THRESHOLD = 106.0
ATTN_TILE = 512
MAX_KEY_BLOCKS = 64

LANES = 128
SUBLANES = 8
MXU_W = 256
VMEM_LIMIT = 56 * 1024 * 1024

F32 = jnp.float32
BF16 = jnp.bfloat16

_dot = functools.partial(jnp.dot, preferred_element_type=F32)


def _dot_nt(a, b):
    return lax.dot_general(a, b, (((1,), (1,)), ((), ())), preferred_element_type=F32)


def _resident(shape):
    nd = len(shape)
    return pl.BlockSpec(shape, lambda *_: (0,) * nd, pipeline_mode=pl.Buffered(1))


def _rows(tm, width):
    return pl.BlockSpec((tm, width), lambda i: (i, 0))


def _params(*sem):
    return pltpu.CompilerParams(dimension_semantics=sem, vmem_limit_bytes=VMEM_LIMIT)


def _rms_mod(x, g, scale, shift):
    ms = jnp.mean(x * x, axis=-1, keepdims=True)
    return (x * lax.rsqrt(ms + RMS_EPS) * g) * (1.0 + scale) + shift


def _shifted(ext, s, halo, tm):
    a, b = divmod(s, SUBLANES)
    r = pltpu.roll(ext, b, axis=0) if b else ext
    lo = halo - a * SUBLANES
    return r[lo:lo + tm, :]


def _split2(z):
    hi = z.astype(BF16)
    lo = (z - hi.astype(F32)).astype(BF16)
    return hi, lo


def _split3(z):
    hi = z.astype(BF16)
    r = z - hi.astype(F32)
    mid = r.astype(BF16)
    lo = (r - mid.astype(F32)).astype(BF16)
    return hi, mid, lo


def _ada_kernel(c_ref, w_ref, b_ref, o_ref):
    cond = jax.nn.silu(c_ref[...])
    o_ref[0, 0] = jnp.sum(w_ref[0] * cond, axis=0, keepdims=True) + b_ref[0, 0]


def _ada(c, ada_w, ada_b):
    depth = ada_w.shape[0]
    out = pl.pallas_call(
        _ada_kernel,
        grid=(depth, 6),
        in_specs=[
            pl.BlockSpec((D, 1), lambda l, j: (0, 0)),
            pl.BlockSpec((1, D, D), lambda l, j: (l, 0, j)),
            pl.BlockSpec((1, 1, 1, D), lambda l, j: (l, j, 0, 0)),
        ],
        out_specs=pl.BlockSpec((1, 1, 1, D), lambda l, j: (l, j, 0, 0)),
        out_shape=jax.ShapeDtypeStruct((depth, 6, 1, D), F32),
        compiler_params=_params("parallel", "parallel"),
        name="ada",
    )(c.reshape(D, 1), ada_w, ada_b.reshape(depth, 6, 1, D))
    return out.reshape(depth, 6, D)


def _ffn_kernel(*refs, tm, cw, pre_proj):
    if pre_proj:
        (x_ref, a_ref, wo_ref, mod_ref, g_ref, wup_ref, wc_ref, wdn_ref, o_ref,
         carry_ref, act_ref) = refs
    else:
        x_ref, mod_ref, g_ref, wup_ref, wc_ref, wdn_ref, o_ref, carry_ref, act_ref = refs

    @pl.when(pl.program_id(0) == 0)
    def _():
        carry_ref[...] = jnp.zeros_like(carry_ref)

    x = x_ref[...]
    if pre_proj:
        x = x + mod_ref[2:3, :] * _dot(a_ref[...], wo_ref[...])
    h = _rms_mod(x, g_ref[...], mod_ref[4:5, :], mod_ref[3:4, :]).astype(BF16)
    for c in range(D_FF // cw):
        lo = c * cw
        u = _dot(h, wup_ref[:, lo:lo + cw])
        gate = _dot(h, wup_ref[:, D_FF + lo:D_FF + lo + cw])
        ext = jnp.concatenate([carry_ref[:, lo:lo + cw], u], axis=0)
        carry_ref[:, lo:lo + cw] = u[tm - SUBLANES:, :]
        conv = (wc_ref[0:1, lo:lo + cw] * _shifted(ext, 2, SUBLANES, tm)
                + wc_ref[1:2, lo:lo + cw] * _shifted(ext, 1, SUBLANES, tm)
                + wc_ref[2:3, lo:lo + cw] * u)
        act_ref[:, lo:lo + cw] = ((conv * jax.nn.sigmoid(conv)) * gate).astype(BF16)
    o_ref[...] = x + mod_ref[5:6, :] * _dot(act_ref[...], wdn_ref[...])


def _layer_resident(shape, layer):
    nd = len(shape)
    return pl.BlockSpec((None,) + tuple(shape), lambda *_: (layer,) + (0,) * nd,
                        pipeline_mode=pl.Buffered(1))


def _ffn(x, mod, g, w_up, w_conv, w_down, layer, pre=None, *, tm=512, cw=256):
    s = x.shape[0]
    ins, specs = [x], [_rows(tm, D)]
    if pre is not None:
        a, w_o = pre
        ins += [a, w_o]
        specs += [_rows(tm, D), _resident((D, D))]
    ins += [mod, g.reshape(1, D), w_up, w_conv, w_down]
    specs += [_resident((6, D)), _resident((1, D)), _layer_resident((D, 2 * D_FF), layer),
              _resident((3, D_FF)), _layer_resident((D_FF, D), layer)]
    return pl.pallas_call(
        functools.partial(_ffn_kernel, tm=tm, cw=cw, pre_proj=pre is not None),
        grid=(s // tm,),
        in_specs=specs,
        out_specs=_rows(tm, D),
        out_shape=jax.ShapeDtypeStruct((s, D), F32),
        scratch_shapes=[pltpu.VMEM((SUBLANES, D_FF), F32), pltpu.VMEM((tm, D_FF), BF16)],
        compiler_params=_params("arbitrary"),
        name="ffn_proj" if pre is not None else "ffn",
    )(*ins)


def _sc_kernel(x_ref, mod_ref, g_ref, win_ref, wc_ref, wout_ref, o_ref, carry_ref, act_ref,
               *, tm, cw):
    @pl.when(pl.program_id(0) == 0)
    def _():
        carry_ref[...] = jnp.zeros_like(carry_ref)

    x = x_ref[...]
    h = _rms_mod(x, g_ref[...], mod_ref[1:2, :], mod_ref[0:1, :]).astype(BF16)
    for c in range(D // cw):
        lo = c * cw
        b_gate = _dot(h, win_ref[:, lo:lo + cw])
        c_gate = _dot(h, win_ref[:, D + lo:D + lo + cw])
        xh = _dot(h, win_ref[:, 2 * D + lo:2 * D + lo + cw])
        u = c_gate * xh
        ext = jnp.concatenate([carry_ref[:, lo:lo + cw], u], axis=0)
        carry_ref[:, lo:lo + cw] = u[tm - SUBLANES:, :]
        conv = (wc_ref[0:1, lo:lo + cw] * _shifted(ext, 2, SUBLANES, tm)
                + wc_ref[1:2, lo:lo + cw] * _shifted(ext, 1, SUBLANES, tm)
                + wc_ref[2:3, lo:lo + cw] * u)
        act_ref[:, lo:lo + cw] = (b_gate * conv).astype(BF16)
    o_ref[...] = x + mod_ref[2:3, :] * _dot(act_ref[...], wout_ref[...])


def _sc_mixer(x, mod, g, w_in, w_conv, w_out, *, tm=512, cw=256):
    s = x.shape[0]
    return pl.pallas_call(
        functools.partial(_sc_kernel, tm=tm, cw=cw),
        grid=(s // tm,),
        in_specs=[_rows(tm, D), _resident((6, D)), _resident((1, D)), _resident((D, 3 * D)),
                  _resident((3, D)), _resident((D, D))],
        out_specs=_rows(tm, D),
        out_shape=jax.ShapeDtypeStruct((s, D), F32),
        scratch_shapes=[pltpu.VMEM((SUBLANES, D), F32), pltpu.VMEM((tm, D), BF16)],
        compiler_params=_params("arbitrary"),
        name="sc_mixer",
    )(x, mod, g.reshape(1, D), w_in, w_conv, w_out)


CF_HALO = 32


def _cf_kernel(x_ref, mod_ref, g_ref, win_ref, wdw_ref, bdw_ref, lng_ref, lnb_ref, wout_ref,
               o_ref, carry_ref, u_ref, *, tm, cw):
    @pl.when(pl.program_id(0) == 0)
    def _():
        carry_ref[...] = jnp.zeros_like(carry_ref)

    x = x_ref[...]
    h = _rms_mod(x, g_ref[...], mod_ref[1:2, :], mod_ref[0:1, :]).astype(BF16)
    for c in range(D // cw):
        lo = c * cw
        a = _dot(h, win_ref[:, lo:lo + cw])
        gate = _dot(h, win_ref[:, D + lo:D + lo + cw])
        glu = a * jax.nn.sigmoid(gate)
        ext = jnp.concatenate([carry_ref[:, lo:lo + cw], glu], axis=0)
        carry_ref[:, lo:lo + cw] = glu[tm - CF_HALO:, :]
        conv = jnp.zeros((tm, cw), F32) + bdw_ref[:, lo:lo + cw]
        for b in range(SUBLANES):
            r = pltpu.roll(ext, b, axis=0) if b else ext
            for a8 in range(CF_HALO // SUBLANES):
                s = a8 * SUBLANES + b
                if s < CONFORMER_K:
                    k = CONFORMER_K - 1 - s
                    start = CF_HALO - a8 * SUBLANES
                    conv = conv + wdw_ref[k:k + 1, lo:lo + cw] * r[start:start + tm, :]
        u_ref[:, lo:lo + cw] = conv
    u = u_ref[...]
    uc = u - jnp.mean(u, axis=-1, keepdims=True)
    y = uc * lax.rsqrt(jnp.mean(uc * uc, axis=-1, keepdims=True) + LN_EPS)
    y = y * lng_ref[...] + lnb_ref[...]
    y = (y * jax.nn.sigmoid(y)).astype(BF16)
    o_ref[...] = x + mod_ref[2:3, :] * _dot(y, wout_ref[...])


def _cf_mixer(x, mod, g, w_in, w_dw, b_dw, ln_g, ln_b, w_out, *, tm=512, cw=256):
    s = x.shape[0]
    return pl.pallas_call(
        functools.partial(_cf_kernel, tm=tm, cw=cw),
        grid=(s // tm,),
        in_specs=[_rows(tm, D), _resident((6, D)), _resident((1, D)), _resident((D, 2 * D)),
                  _resident((CONFORMER_K, D)), _resident((1, D)), _resident((1, D)),
                  _resident((1, D)), _resident((D, D))],
        out_specs=_rows(tm, D),
        out_shape=jax.ShapeDtypeStruct((s, D), F32),
        scratch_shapes=[pltpu.VMEM((CF_HALO, D), F32), pltpu.VMEM((tm, D), F32)],
        compiler_params=_params("arbitrary"),
        name="cf_mixer",
    )(x, mod, g.reshape(1, D), w_in, w_dw, b_dw.reshape(1, D), ln_g.reshape(1, D),
      ln_b.reshape(1, D), w_out)


def _group_mean_matrix(n):
    r = lax.broadcasted_iota(jnp.int32, (n, n), 0) // HEAD_DIM
    c = lax.broadcasted_iota(jnp.int32, (n, n), 1) // HEAD_DIM
    return jnp.where(r == c, 1.0 / HEAD_DIM, 0.0).astype(BF16)


def _head_rms(z, gm):
    hi, lo = _split2(z * z)
    ms = _dot(hi, gm) + _dot(lo, gm)
    return z * lax.rsqrt(ms + RMS_EPS)


def _store_v_with_ones(v, v0_ref, v1_ref, lo, cw):
    lane = lax.broadcasted_iota(jnp.int32, v.shape, 1)
    first = (lane & (LANES - 1)) < HEAD_DIM
    v0_ref[:, lo:lo + cw] = jnp.where(first, v, 1.0).astype(BF16)
    v1_ref[:, lo:lo + cw] = jnp.where(first, 1.0, v).astype(BF16)


def _fox_proj_kernel(x_ref, mod_ref, g_ref, w_ref, bf_ref, qn_ref, kn_ref,
                     q_ref, k_ref, v0_ref, v1_ref, kx_ref, f_ref, carry_ref, tri_ref, *, tm, cw):
    @pl.when(pl.program_id(0) == 0)
    def _():
        carry_ref[...] = jnp.zeros_like(carry_ref)
        r = lax.broadcasted_iota(jnp.int32, (tm, tm), 0)
        c = lax.broadcasted_iota(jnp.int32, (tm, tm), 1)
        tri_ref[...] = jnp.where(r >= c, 1.0, 0.0).astype(BF16)

    h = _rms_mod(x_ref[...], g_ref[...], mod_ref[1:2, :], mod_ref[0:1, :]).astype(BF16)
    gm = _group_mean_matrix(cw)
    for c in range(D // cw):
        lo = c * cw
        q = _head_rms(_dot(h, w_ref[:, lo:lo + cw]), gm)
        q_ref[:, lo:lo + cw] = (q * qn_ref[:, lo:lo + cw] * (SCALE * LOG2E)).astype(BF16)
        k = _head_rms(_dot(h, w_ref[:, D + lo:D + lo + cw]), gm)
        k_ref[:, lo:lo + cw] = (k * kn_ref[:, lo:lo + cw]).astype(BF16)
        _store_v_with_ones(_dot(h, w_ref[:, 2 * D + lo:2 * D + lo + cw]), v0_ref, v1_ref, lo, cw)
    fl = _dot(h, w_ref[:, 3 * D:3 * D + LANES]) + bf_ref[...]
    lf = jnp.minimum(fl, 0.0) - jnp.log1p(jnp.exp(-jnp.abs(fl)))
    tri = tri_ref[...]
    hi, mid, lo3 = _split3(lf)
    local = _dot(tri, hi) + _dot(tri, mid) + _dot(tri, lo3)
    cum = local + carry_ref[0:1, :]
    f_ref[...] = cum
    carry_ref[...] = jnp.broadcast_to(cum[tm - 1:tm, :], carry_ref.shape)
    head = lax.broadcasted_iota(jnp.int32, (LANES, D), 0)
    lane = lax.broadcasted_iota(jnp.int32, (LANES, D), 1)
    target = lax.shift_right_logical(head, 1) * LANES + (head & 1) * 3
    kx = jnp.zeros((tm, D), F32)
    for t, piece in enumerate(_split3(local * (-LOG2E))):
        place = jnp.where((lane == target + t) & (head < N_HEADS), 1.0, 0.0).astype(BF16)
        kx = kx + _dot(piece, place)
    kx_ref[...] = kx.astype(BF16)


def _fox_proj(x, mod, g, w_in, b_f, qn, kn, *, tm=ATTN_TILE, cw=256):
    s = x.shape[0]
    w = jnp.pad(w_in, ((0, 0), (0, LANES - N_HEADS))).astype(BF16)
    bf = jnp.pad(b_f, (0, LANES - N_HEADS)).reshape(1, LANES)
    return pl.pallas_call(
        functools.partial(_fox_proj_kernel, tm=tm, cw=cw),
        grid=(s // tm,),
        in_specs=[_rows(tm, D), _resident((6, D)), _resident((1, D)),
                  _resident((D, 3 * D + LANES)), _resident((1, LANES)),
                  _resident((1, D)), _resident((1, D))],
        out_specs=[_rows(tm, D)] * 5 + [_rows(tm, LANES)],
        out_shape=[jax.ShapeDtypeStruct((s, D), BF16)] * 5 + [jax.ShapeDtypeStruct((s, LANES), F32)],
        scratch_shapes=[pltpu.VMEM((SUBLANES, LANES), F32), pltpu.VMEM((tm, tm), BF16)],
        compiler_params=_params("arbitrary"),
        name="fox_proj",
    )(x, mod, g.reshape(1, D), w, bf, jnp.tile(qn, N_HEADS).reshape(1, D),
      jnp.tile(kn, N_HEADS).reshape(1, D))


def _moba_proj_kernel(x_ref, mod_ref, g_ref, w_ref, qn_ref, kn_ref,
                      q_ref, k_ref, v0_ref, v1_ref, qf_ref, km_ref, kx_ref, *, tm, cw):
    row = lax.broadcasted_iota(jnp.int32, (tm, LANES), 0)
    lane = lax.broadcasted_iota(jnp.int32, (tm, LANES), 1)
    rem = row & (MOBA_BLOCK - 1)
    block = pl.program_id(0) * (tm // MOBA_BLOCK) + lax.shift_right_logical(row, MOBA_BLOCK.bit_length() - 1)
    offset_lanes = jnp.where(lane < MAX_KEY_BLOCKS + 3, rem, row - rem).astype(F32)
    kx_ref[...] = jnp.where(
        lane < MAX_KEY_BLOCKS, jnp.where(lane == block, 1.0, 0.0),
        jnp.where(lane < MAX_KEY_BLOCKS + 6, offset_lanes, 0.0)).astype(BF16)

    h = _rms_mod(x_ref[...], g_ref[...], mod_ref[1:2, :], mod_ref[0:1, :]).astype(BF16)
    gm = _group_mean_matrix(cw)
    for c in range(D // cw):
        lo = c * cw
        q = _head_rms(_dot(h, w_ref[:, lo:lo + cw]), gm) * qn_ref[:, lo:lo + cw]
        qf_ref[:, lo:lo + cw] = q
        q_ref[:, lo:lo + cw] = (q * (SCALE * LOG2E)).astype(BF16)
        k = _head_rms(_dot(h, w_ref[:, D + lo:D + lo + cw]), gm) * kn_ref[:, lo:lo + cw]
        k_ref[:, lo:lo + cw] = k.astype(BF16)
        for b in range(tm // MOBA_BLOCK):
            km_ref[b, :, lo:lo + cw] = jnp.mean(
                k[b * MOBA_BLOCK:(b + 1) * MOBA_BLOCK, :], axis=0, keepdims=True)
        _store_v_with_ones(_dot(h, w_ref[:, 2 * D + lo:2 * D + lo + cw]), v0_ref, v1_ref, lo, cw)


def _moba_proj(x, mod, g, w_in, qn, kn, *, tm=ATTN_TILE, cw=256):
    s = x.shape[0]
    nb = tm // MOBA_BLOCK
    return pl.pallas_call(
        functools.partial(_moba_proj_kernel, tm=tm, cw=cw),
        grid=(s // tm,),
        in_specs=[_rows(tm, D), _resident((6, D)), _resident((1, D)), _resident((D, 3 * D)),
                  _resident((1, D)), _resident((1, D))],
        out_specs=[_rows(tm, D)] * 5 + [pl.BlockSpec((nb, 1, D), lambda i: (i, 0, 0)),
                                        _rows(tm, LANES)],
        out_shape=[jax.ShapeDtypeStruct((s, D), BF16)] * 4
        + [jax.ShapeDtypeStruct((s, D), F32), jax.ShapeDtypeStruct((s // MOBA_BLOCK, 1, D), F32),
           jax.ShapeDtypeStruct((s, LANES), BF16)],
        compiler_params=_params("parallel"),
        name="moba_proj",
    )(x, mod, g.reshape(1, D), w_in, jnp.tile(qn, N_HEADS).reshape(1, D),
      jnp.tile(kn, N_HEADS).reshape(1, D))


def _moba_gate_kernel(q_ref, km_ref, slope_ref, qx_ref):
    own = pl.program_id(0)
    t = q_ref.shape[0]
    blk = lax.broadcasted_iota(jnp.int32, (MAX_KEY_BLOCKS, t), 0)
    lane = lax.broadcasted_iota(jnp.int32, (t, LANES), 1)
    km_lane_head = lax.shift_right_logical(
        lax.broadcasted_iota(jnp.int32, (MAX_KEY_BLOCKS, MXU_W), 1), HEAD_DIM.bit_length() - 1)
    pad_rows = jnp.full((LANES - MAX_KEY_BLOCKS, t), NEG_INF, F32)
    for g4 in range(D // MXU_W):
        lo = g4 * MXU_W
        q_hi, q_lo = _split2(q_ref[:, lo:lo + MXU_W])
        km = km_ref[:, lo:lo + MXU_W]
        for hl in range(MXU_W // HEAD_DIM):
            head = 4 * g4 + hl
            km_hi, km_lo = _split2(jnp.where(km_lane_head == hl, km, 0.0))
            gate = _dot_nt(km_hi, q_hi) + _dot_nt(km_lo, q_hi) + _dot_nt(km_hi, q_lo)
            keep = blk == own
            gh = jnp.where(blk < own, gate, NEG_INF)
            for _ in range(MOBA_TOPK):
                mx = jnp.max(gh, axis=0, keepdims=True)
                idx = jnp.min(jnp.where(gh == mx, blk, MAX_KEY_BLOCKS), axis=0, keepdims=True)
                pick = blk == idx
                keep = keep | (pick & (idx < own))
                gh = jnp.where(pick, -jnp.inf, gh)
            bias = jnp.concatenate([jnp.where(keep, 0.0, NEG_INF), pad_rows], axis=0).T
            qx = jnp.where(lane < MAX_KEY_BLOCKS, bias, slope_ref[head:head + 1, :])
            qx_ref[:, head * LANES:(head + 1) * LANES] = qx.astype(BF16)


def _moba_gate(qf, kmean, slope_lanes):
    s = qf.shape[0]
    n_kb = s // MOBA_BLOCK
    assert n_kb <= MAX_KEY_BLOCKS
    km = jnp.pad(kmean.reshape(n_kb, D), ((0, MAX_KEY_BLOCKS - n_kb), (0, 0)))
    return pl.pallas_call(
        _moba_gate_kernel,
        grid=(n_kb,),
        in_specs=[_rows(MOBA_BLOCK, D), _resident((MAX_KEY_BLOCKS, D)), _resident((N_HEADS, LANES))],
        out_specs=_rows(MOBA_BLOCK, N_HEADS * LANES),
        out_shape=jax.ShapeDtypeStruct((s, N_HEADS * LANES), BF16),
        compiler_params=_params("parallel"),
        name="moba_gate",
    )(qf, km, slope_lanes)


def _attn_kernel(*refs, t, moba):
    if moba:
        nvis_ref, r_ref, q_ref, kk_ref, kx_ref, v0_ref, v1_ref, qx_ref, o_ref, acc_ref, m_ref, s_ref, mp_ref = refs
    else:
        nvis_ref, r_ref, q_ref, kk_ref, kx_ref, v0_ref, v1_ref, o_ref, acc_ref, m_ref, s_ref, mp_ref = refs
    pair = pl.program_id(0)
    i = pl.program_id(1)
    q = q_ref[...]
    lane = lax.broadcasted_iota(jnp.int32, (t, LANES), 1)
    row = lax.broadcasted_iota(jnp.int32, (t, t), 0)
    col = lax.broadcasted_iota(jnp.int32, (t, t), 1)
    v_refs = (v0_ref, v1_ref)
    n_past = [nvis_ref[2 * pair + hh, i] for hh in range(2)]

    def query(hh):
        mine = (lane < HEAD_DIM) if hh == 0 else (lane >= HEAD_DIM)
        if moba:
            qx = qx_ref[:, hh * LANES:(hh + 1) * LANES]
        else:
            qx = jnp.where((lane >= 3 * hh) & (lane < 3 * hh + 3), 1.0, 0.0).astype(BF16)
        return jnp.concatenate([jnp.where(mine, q, jnp.zeros_like(q)), qx], axis=1)

    qh = [query(hh) for hh in range(2)]

    def scores(hh, j):
        start = pl.multiple_of(j * t, t)
        kk = jnp.concatenate([kk_ref[pl.ds(start, t), :], kx_ref[pl.ds(start, t), :]], axis=1)
        return _dot_nt(qh[hh], kk)

    def values(hh, j):
        return v_refs[hh][pl.ds(pl.multiple_of(j * t, t), t), :]

    def across(stat):
        return jnp.concatenate([stat] * (t // LANES), axis=1)

    def produce(hh, j, slot):
        s = scores(hh, j)
        s_ref[hh, slot] = s
        part = s[:, 0:LANES]
        for g in range(1, t // LANES):
            part = jnp.maximum(part, s[:, g * LANES:(g + 1) * LANES])
        mp_ref[hh, slot] = part

    def step(hh, j, cur, nxt=None):
        h = 2 * pair + hh
        if nxt is not None:
            produce(hh, jnp.maximum(j - 1, 0), nxt)
        s = s_ref[hh, cur]
        if moba:
            dist = jnp.zeros((1, 1), jnp.int32) + (i - j) * t
            c = -(r_ref[h, 0] * dist.astype(F32))
        else:
            c = (jnp.zeros((1, 1), F32) + r_ref[h, i]) - (jnp.zeros((1, 1), F32) + r_ref[h, j])
        m_old = m_ref[hh]
        m_tile = jnp.broadcast_to(jnp.max(mp_ref[hh, cur], axis=1, keepdims=True), (t, LANES))
        m_new = jnp.maximum(m_old, m_tile + c)
        alpha = jnp.exp2(m_old - m_new)
        p = jnp.exp2(s - across(m_new - c))
        acc_ref[hh] = alpha * acc_ref[hh] + _dot(p.astype(BF16), values(hh, j))
        m_ref[hh] = m_new

    for hh in range(2):
        s = jnp.where(col <= row, scores(hh, i), NEG_INF)
        m = jnp.broadcast_to(jnp.max(s, axis=1, keepdims=True), (t, LANES))
        acc_ref[hh] = _dot(jnp.exp2(s - across(m)).astype(BF16), values(hh, i))
        m_ref[hh] = m
        produce(hh, jnp.maximum(i - 1, 0), 0)

    n_joint = jnp.minimum(n_past[0], n_past[1]) // 2

    def joint_body(k2, carry):
        j = i - 1 - 2 * k2
        for hh in range(2):
            step(hh, j, 0, 1)
        for hh in range(2):
            step(hh, j - 1, 1, 0)
        return carry

    lax.fori_loop(0, n_joint, joint_body, 0)

    outs = []
    for hh in range(2):
        def tail_body(k2, carry, hh=hh):
            j = i - 1 - 2 * k2
            step(hh, j, 0, 1)
            step(hh, j - 1, 1, 0)
            return carry

        n_pairs = n_past[hh] // 2
        lax.fori_loop(n_joint, n_pairs, tail_body, 0)

        @pl.when(n_past[hh] % 2 == 1)
        def _(hh=hh, n_pairs=n_pairs):
            step(hh, i - 1 - 2 * n_pairs, 0)

        acc = acc_ref[hh]
        outs.append(acc * (1.0 / pltpu.roll(acc, HEAD_DIM, axis=1)))
    o_ref[...] = jnp.where(lane < HEAD_DIM, outs[0], outs[1]).astype(BF16)


def _attention(nvis, r, q, kk, kx, v0, v1, qx=None, *, t):
    s = q.shape[0]
    moba = qx is not None
    col_block = pl.BlockSpec((s, LANES), lambda p, i, *_: (0, p), pipeline_mode=pl.Buffered(1))
    tile = pl.BlockSpec((t, LANES), lambda p, i, *_: (i, p))
    ins = [q, kk, kx, v0, v1]
    if moba:
        kx_spec = pl.BlockSpec((s, LANES), lambda p, i, *_: (0, 0), pipeline_mode=pl.Buffered(1))
        specs = [tile, col_block, kx_spec, col_block, col_block,
                 pl.BlockSpec((t, 2 * LANES), lambda p, i, *_: (i, p))]
        ins.append(qx)
    else:
        specs = [tile, col_block, col_block, col_block, col_block]
    return pl.pallas_call(
        functools.partial(_attn_kernel, t=t, moba=moba),
        grid_spec=pltpu.PrefetchScalarGridSpec(
            num_scalar_prefetch=2,
            grid=(N_PAIRS, s // t),
            in_specs=specs,
            out_specs=tile,
            scratch_shapes=[pltpu.VMEM((2, t, LANES), F32), pltpu.VMEM((2, t, LANES), F32),
                            pltpu.VMEM((2, 2, t, t), F32), pltpu.VMEM((2, 2, t, LANES), F32)],
        ),
        out_shape=jax.ShapeDtypeStruct((s, D), BF16),
        compiler_params=_params("parallel", "parallel"),
        name="moba_attn" if moba else "fox_attn",
    )(nvis, r, *ins)


def _first_visible(vis):
    nt = vis.shape[1]
    i = jnp.arange(nt)[None, :, None]
    j = jnp.arange(nt)[None, None, :]
    first = jnp.min(jnp.where(vis & (j < i), j, i), axis=2)
    return (i[:, :, 0] - first).astype(jnp.int32)


def _score_bound(qn, kn):
    return 1.02 * HEAD_DIM ** 0.5 * jnp.max(jnp.abs(qn)) * jnp.max(jnp.abs(kn))


def _fox_schedule(cum_f, qn, kn, t):
    f = cum_f[:, :N_HEADS].T
    f_before = jnp.concatenate([jnp.zeros((N_HEADS, 1), F32), f[:, t - 1:-1:t]], axis=1)
    reach = SKIP_THRESHOLD + 2.0 * _score_bound(qn, kn)
    vis = (f[:, 0::t, None] - f[:, None, t - 1::t]) >= -reach
    return _first_visible(vis), LOG2E * f_before


def _moba_constants(s, qn, kn, t):
    slopes = jnp.exp2(-ALIBI_MAX_EXP * jnp.arange(1, N_HEADS + 1, dtype=F32) / N_HEADS)
    hi, mid, lo = _split3(slopes * LOG2E)
    pieces = jnp.stack([hi, mid, lo, hi, mid, lo], axis=1).astype(F32)
    slope_lanes = jnp.zeros((N_HEADS, LANES), F32).at[:, MAX_KEY_BLOCKS:MAX_KEY_BLOCKS + 6].set(pieces)
    nt = s // t
    i = jnp.arange(nt)[None, :, None]
    j = jnp.arange(nt)[None, None, :]
    nearest = ((i - j - 1) * t + 1).astype(F32)
    reach = SKIP_THRESHOLD + 2.0 * _score_bound(qn, kn)
    nvis = _first_visible(slopes[:, None, None] * nearest <= reach)
    r = jnp.broadcast_to((slopes * LOG2E)[:, None], (N_HEADS, nt))
    return slope_lanes, nvis, r


def kernel(x, c, ln_mix_g, ln_ffn_g, ada_w, ada_b, ffn_up, ffn_conv, ffn_down, sc_in, sc_conv, sc_out, cf_in, cf_dw, cf_dw_b, cf_ln_g, cf_ln_b, cf_out, fox_in, fox_bf, fox_qn, fox_kn, fox_out, moba_in, moba_qn, moba_kn, moba_out):
    batch, s, _ = x.shape
    assert batch == 1 and c.shape[0] == 1
    depth = ada_w.shape[0]
    mods = _ada(c, ada_w, ada_b)
    ffn_up_bf16 = ffn_up.astype(BF16)
    ffn_down_bf16 = ffn_down.astype(BF16)
    xs = x.reshape(s, D)
    for i in range(depth):
        kind, j = i % 4, i // 4
        mod = mods[i]
        pre = None
        if kind == 0:
            xs = _sc_mixer(xs, mod, ln_mix_g[i], sc_in[j].astype(BF16), sc_conv[j],
                           sc_out[j].astype(BF16))
        elif kind == 1:
            xs = _cf_mixer(xs, mod, ln_mix_g[i], cf_in[j].astype(BF16), cf_dw[j], cf_dw_b[j],
                           cf_ln_g[j], cf_ln_b[j], cf_out[j].astype(BF16))
        elif kind == 2:
            q, k, v0, v1, kx, cum_f = _fox_proj(xs, mod, ln_mix_g[i], fox_in[j], fox_bf[j],
                                                fox_qn[j], fox_kn[j])
            nvis, r = _fox_schedule(cum_f, fox_qn[j], fox_kn[j], ATTN_TILE)
            pre = (_attention(nvis, r, q, k, kx, v0, v1, t=ATTN_TILE), fox_out[j].astype(BF16))
        else:
            q, k, v0, v1, qf, kmean, kx = _moba_proj(xs, mod, ln_mix_g[i], moba_in[j].astype(BF16),
                                                     moba_qn[j], moba_kn[j])
            slope_lanes, nvis, r = _moba_constants(s, moba_qn[j], moba_kn[j], ATTN_TILE)
            qx = _moba_gate(qf, kmean, slope_lanes)
            pre = (_attention(nvis, r, q, k, kx, v0, v1, qx, t=ATTN_TILE), moba_out[j].astype(BF16))
        xs = _ffn(xs, mod, ln_ffn_g[i], ffn_up_bf16, ffn_conv[i], ffn_down_bf16, i, pre)
    return xs.reshape(1, s, D)
```

```python
import functools

import jax
import jax.numpy as jnp
from jax import lax
from jax.experimental import pallas as pl
from jax.experimental.pallas import tpu as pltpu

D = 1024
N_HEADS = 16
HEAD_DIM = D // N_HEADS
N_PAIRS = N_HEADS // 2
D_FF = ((8 * D // 3 + 127) // 128) * 128
CONFORMER_K = 31
MOBA_BLOCK = 256
MOBA_TOPK = 3
ALIBI_MAX_EXP = 8.0
RMS_EPS = 1e-6
LN_EPS = 1e-5
NEG_INF = -1e30
SCALE = HEAD_DIM ** -0.5
LOG2E = 1.4426950408889634
SKIP_THRESHOLD = 106.0
ATTN_TILE = 512
MAX_KEY_BLOCKS = 64

LANES = 128
SUBLANES = 8
MXU_W = 256
VMEM_LIMIT = 56 * 1024 * 1024

F32 = jnp.float32
BF16 = jnp.bfloat16

_dot = functools.partial(jnp.dot, preferred_element_type=F32)


def _dot_nt(a, b):
    return lax.dot_general(a, b, (((1,), (1,)), ((), ())), preferred_element_type=F32)


def _resident(shape):
    nd = len(shape)
    return pl.BlockSpec(shape, lambda *_: (0,) * nd, pipeline_mode=pl.Buffered(1))


def _rows(tm, width):
    return pl.BlockSpec((tm, width), lambda i: (i, 0))


def _cols(tm):
    return pl.BlockSpec((D, tm), lambda i: (0, i))


def _params(*sem):
    return pltpu.CompilerParams(dimension_semantics=sem, vmem_limit_bytes=VMEM_LIMIT)


def _rms_mod(x, g, scale, shift):
    ms = jnp.mean(x * x, axis=-1, keepdims=True)
    return (x * lax.rsqrt(ms + RMS_EPS) * g) * (1.0 + scale) + shift


def _shifted(ext, s, halo, tm):
    a, b = divmod(s, SUBLANES)
    r = pltpu.roll(ext, b, axis=0) if b else ext
    lo = halo - a * SUBLANES
    return r[lo:lo + tm, :]


def _split2(z):
    hi = z.astype(BF16)
    lo = (z - hi.astype(F32)).astype(BF16)
    return hi, lo


def _split3(z):
    hi = z.astype(BF16)
    r = z - hi.astype(F32)
    mid = r.astype(BF16)
    lo = (r - mid.astype(F32)).astype(BF16)
    return hi, mid, lo


def _ada_kernel(c_ref, w_ref, b_ref, o_ref):
    cond = jax.nn.silu(c_ref[...])
    o_ref[0, 0] = jnp.sum(w_ref[0] * cond, axis=0, keepdims=True) + b_ref[0, 0]


def _ada(c, ada_w, ada_b):
    depth = ada_w.shape[0]
    out = pl.pallas_call(
        _ada_kernel,
        grid=(depth, 6),
        in_specs=[
            pl.BlockSpec((D, 1), lambda l, j: (0, 0)),
            pl.BlockSpec((1, D, D), lambda l, j: (l, 0, j)),
            pl.BlockSpec((1, 1, 1, D), lambda l, j: (l, j, 0, 0)),
        ],
        out_specs=pl.BlockSpec((1, 1, 1, D), lambda l, j: (l, j, 0, 0)),
        out_shape=jax.ShapeDtypeStruct((depth, 6, 1, D), F32),
        compiler_params=_params("parallel", "parallel"),
        name="ada",
    )(c.reshape(D, 1), ada_w, ada_b.reshape(depth, 6, 1, D))
    return out.reshape(depth, 6, D)


def _ffn_kernel(*refs, tm, cw, pre_proj):
    if pre_proj:
        (x_ref, a_ref, wo_ref, mod_ref, g_ref, wup_ref, wc_ref, wdn_ref, o_ref,
         carry_ref, act_ref) = refs
    else:
        x_ref, mod_ref, g_ref, wup_ref, wc_ref, wdn_ref, o_ref, carry_ref, act_ref = refs

    @pl.when(pl.program_id(0) == 0)
    def _():
        carry_ref[...] = jnp.zeros_like(carry_ref)

    x = x_ref[...]
    if pre_proj:
        x = x + mod_ref[2:3, :] * _dot(a_ref[...], wo_ref[...])
    h = _rms_mod(x, g_ref[...], mod_ref[4:5, :], mod_ref[3:4, :]).astype(BF16)
    for c in range(D_FF // cw):
        lo = c * cw
        u = _dot(h, wup_ref[:, lo:lo + cw])
        gate = _dot(h, wup_ref[:, D_FF + lo:D_FF + lo + cw])
        ext = jnp.concatenate([carry_ref[:, lo:lo + cw], u], axis=0)
        carry_ref[:, lo:lo + cw] = u[tm - SUBLANES:, :]
        conv = (wc_ref[0:1, lo:lo + cw] * _shifted(ext, 2, SUBLANES, tm)
                + wc_ref[1:2, lo:lo + cw] * _shifted(ext, 1, SUBLANES, tm)
                + wc_ref[2:3, lo:lo + cw] * u)
        act_ref[:, lo:lo + cw] = ((conv * jax.nn.sigmoid(conv)) * gate).astype(BF16)
    o_ref[...] = x + mod_ref[5:6, :] * _dot(act_ref[...], wdn_ref[...])


def _layer_resident(shape, layer):
    nd = len(shape)
    return pl.BlockSpec((None,) + tuple(shape), lambda *_: (layer,) + (0,) * nd,
                        pipeline_mode=pl.Buffered(1))


def _ffn(x, mod, g, w_up, w_conv, w_down, layer, pre=None, *, tm=512, cw=256):
    s = x.shape[0]
    ins, specs = [x], [_rows(tm, D)]
    if pre is not None:
        a, w_o = pre
        ins += [a, w_o]
        specs += [_rows(tm, D), _resident((D, D))]
    ins += [mod, g.reshape(1, D), w_up, w_conv, w_down]
    specs += [_resident((6, D)), _resident((1, D)), _layer_resident((D, 2 * D_FF), layer),
              _resident((3, D_FF)), _layer_resident((D_FF, D), layer)]
    return pl.pallas_call(
        functools.partial(_ffn_kernel, tm=tm, cw=cw, pre_proj=pre is not None),
        grid=(s // tm,),
        in_specs=specs,
        out_specs=_rows(tm, D),
        out_shape=jax.ShapeDtypeStruct((s, D), F32),
        scratch_shapes=[pltpu.VMEM((SUBLANES, D_FF), F32), pltpu.VMEM((tm, D_FF), BF16)],
        compiler_params=_params("arbitrary"),
        name="ffn_proj" if pre is not None else "ffn",
    )(*ins)


def _sc_kernel(x_ref, mod_ref, g_ref, win_ref, wc_ref, wout_ref, o_ref, carry_ref, act_ref,
               *, tm, cw):
    @pl.when(pl.program_id(0) == 0)
    def _():
        carry_ref[...] = jnp.zeros_like(carry_ref)

    x = x_ref[...]
    h = _rms_mod(x, g_ref[...], mod_ref[1:2, :], mod_ref[0:1, :]).astype(BF16)
    for c in range(D // cw):
        lo = c * cw
        b_gate = _dot(h, win_ref[:, lo:lo + cw])
        c_gate = _dot(h, win_ref[:, D + lo:D + lo + cw])
        xh = _dot(h, win_ref[:, 2 * D + lo:2 * D + lo + cw])
        u = c_gate * xh
        ext = jnp.concatenate([carry_ref[:, lo:lo + cw], u], axis=0)
        carry_ref[:, lo:lo + cw] = u[tm - SUBLANES:, :]
        conv = (wc_ref[0:1, lo:lo + cw] * _shifted(ext, 2, SUBLANES, tm)
                + wc_ref[1:2, lo:lo + cw] * _shifted(ext, 1, SUBLANES, tm)
                + wc_ref[2:3, lo:lo + cw] * u)
        act_ref[:, lo:lo + cw] = (b_gate * conv).astype(BF16)
    o_ref[...] = x + mod_ref[2:3, :] * _dot(act_ref[...], wout_ref[...])


def _sc_mixer(x, mod, g, w_in, w_conv, w_out, *, tm=512, cw=256):
    s = x.shape[0]
    return pl.pallas_call(
        functools.partial(_sc_kernel, tm=tm, cw=cw),
        grid=(s // tm,),
        in_specs=[_rows(tm, D), _resident((6, D)), _resident((1, D)), _resident((D, 3 * D)),
                  _resident((3, D)), _resident((D, D))],
        out_specs=_rows(tm, D),
        out_shape=jax.ShapeDtypeStruct((s, D), F32),
        scratch_shapes=[pltpu.VMEM((SUBLANES, D), F32), pltpu.VMEM((tm, D), BF16)],
        compiler_params=_params("arbitrary"),
        name="sc_mixer",
    )(x, mod, g.reshape(1, D), w_in, w_conv, w_out)


CF_HALO = 32


def _cf_kernel(x_ref, mod_ref, g_ref, win_ref, wdw_ref, bdw_ref, lng_ref, lnb_ref, wout_ref,
               o_ref, carry_ref, u_ref, *, tm, cw):
    @pl.when(pl.program_id(0) == 0)
    def _():
        carry_ref[...] = jnp.zeros_like(carry_ref)

    x = x_ref[...]
    h = _rms_mod(x, g_ref[...], mod_ref[1:2, :], mod_ref[0:1, :]).astype(BF16)
    for c in range(D // cw):
        lo = c * cw
        a = _dot(h, win_ref[:, lo:lo + cw])
        gate = _dot(h, win_ref[:, D + lo:D + lo + cw])
        glu = a * jax.nn.sigmoid(gate)
        ext = jnp.concatenate([carry_ref[:, lo:lo + cw], glu], axis=0)
        carry_ref[:, lo:lo + cw] = glu[tm - CF_HALO:, :]
        conv = jnp.zeros((tm, cw), F32) + bdw_ref[:, lo:lo + cw]
        for b in range(SUBLANES):
            r = pltpu.roll(ext, b, axis=0) if b else ext
            for a8 in range(CF_HALO // SUBLANES):
                s = a8 * SUBLANES + b
                if s < CONFORMER_K:
                    k = CONFORMER_K - 1 - s
                    start = CF_HALO - a8 * SUBLANES
                    conv = conv + wdw_ref[k:k + 1, lo:lo + cw] * r[start:start + tm, :]
        u_ref[:, lo:lo + cw] = conv
    u = u_ref[...]
    uc = u - jnp.mean(u, axis=-1, keepdims=True)
    y = uc * lax.rsqrt(jnp.mean(uc * uc, axis=-1, keepdims=True) + LN_EPS)
    y = y * lng_ref[...] + lnb_ref[...]
    y = (y * jax.nn.sigmoid(y)).astype(BF16)
    o_ref[...] = x + mod_ref[2:3, :] * _dot(y, wout_ref[...])


def _cf_mixer(x, mod, g, w_in, w_dw, b_dw, ln_g, ln_b, w_out, *, tm=512, cw=256):
    s = x.shape[0]
    return pl.pallas_call(
        functools.partial(_cf_kernel, tm=tm, cw=cw),
        grid=(s // tm,),
        in_specs=[_rows(tm, D), _resident((6, D)), _resident((1, D)), _resident((D, 2 * D)),
                  _resident((CONFORMER_K, D)), _resident((1, D)), _resident((1, D)),
                  _resident((1, D)), _resident((D, D))],
        out_specs=_rows(tm, D),
        out_shape=jax.ShapeDtypeStruct((s, D), F32),
        scratch_shapes=[pltpu.VMEM((CF_HALO, D), F32), pltpu.VMEM((tm, D), F32)],
        compiler_params=_params("arbitrary"),
        name="cf_mixer",
    )(x, mod, g.reshape(1, D), w_in, w_dw, b_dw.reshape(1, D), ln_g.reshape(1, D),
      ln_b.reshape(1, D), w_out)


def _group_mean_matrix(n):
    r = lax.broadcasted_iota(jnp.int32, (n, n), 0) // HEAD_DIM
    c = lax.broadcasted_iota(jnp.int32, (n, n), 1) // HEAD_DIM
    return jnp.where(r == c, 1.0 / HEAD_DIM, 0.0).astype(BF16)


def _head_rms(z, gm):
    hi, lo = _split2(z * z)
    ms = _dot(hi, gm) + _dot(lo, gm)
    return z * lax.rsqrt(ms + RMS_EPS)


def _store_v_with_ones(v, v0_ref, v1_ref, lo, cw):
    vt = v.T
    feature = lax.broadcasted_iota(jnp.int32, vt.shape, 0)
    first = (feature & (LANES - 1)) < HEAD_DIM
    v0_ref[lo:lo + cw, :] = jnp.where(first, vt, 1.0).astype(BF16)
    v1_ref[lo:lo + cw, :] = jnp.where(first, 1.0, vt).astype(BF16)


def _fox_proj_kernel(x_ref, mod_ref, g_ref, w_ref, bf_ref, qn_ref, kn_ref,
                     q_ref, k_ref, v0_ref, v1_ref, kx_ref, f_ref, carry_ref, tri_ref, *, tm, cw):
    @pl.when(pl.program_id(0) == 0)
    def _():
        carry_ref[...] = jnp.zeros_like(carry_ref)
        r = lax.broadcasted_iota(jnp.int32, (tm, tm), 0)
        c = lax.broadcasted_iota(jnp.int32, (tm, tm), 1)
        tri_ref[...] = jnp.where(r >= c, 1.0, 0.0).astype(BF16)

    h = _rms_mod(x_ref[...], g_ref[...], mod_ref[1:2, :], mod_ref[0:1, :]).astype(BF16)
    gm = _group_mean_matrix(cw)
    for c in range(D // cw):
        lo = c * cw
        q = _head_rms(_dot(h, w_ref[:, lo:lo + cw]), gm)
        q_ref[:, lo:lo + cw] = (q * qn_ref[:, lo:lo + cw] * (SCALE * LOG2E)).astype(BF16)
        k = _head_rms(_dot(h, w_ref[:, D + lo:D + lo + cw]), gm)
        k_ref[:, lo:lo + cw] = (k * kn_ref[:, lo:lo + cw]).astype(BF16)
        _store_v_with_ones(_dot(h, w_ref[:, 2 * D + lo:2 * D + lo + cw]), v0_ref, v1_ref, lo, cw)
    fl = _dot(h, w_ref[:, 3 * D:3 * D + LANES]) + bf_ref[...]
    lf = jnp.minimum(fl, 0.0) - jnp.log1p(jnp.exp(-jnp.abs(fl)))
    tri = tri_ref[...]
    hi, mid, lo3 = _split3(lf)
    local = _dot(tri, hi) + _dot(tri, mid) + _dot(tri, lo3)
    cum = local + carry_ref[0:1, :]
    f_ref[...] = cum
    carry_ref[...] = jnp.broadcast_to(cum[tm - 1:tm, :], carry_ref.shape)
    head = lax.broadcasted_iota(jnp.int32, (LANES, D), 0)
    lane = lax.broadcasted_iota(jnp.int32, (LANES, D), 1)
    target = lax.shift_right_logical(head, 1) * LANES + (head & 1) * 3
    kx = jnp.zeros((tm, D), F32)
    for t, piece in enumerate(_split3(local * (-LOG2E))):
        place = jnp.where((lane == target + t) & (head < N_HEADS), 1.0, 0.0).astype(BF16)
        kx = kx + _dot(piece, place)
    kx_ref[...] = kx.astype(BF16)


def _fox_proj(x, mod, g, w_in, b_f, qn, kn, *, tm=ATTN_TILE, cw=256):
    s = x.shape[0]
    w = jnp.pad(w_in, ((0, 0), (0, LANES - N_HEADS))).astype(BF16)
    bf = jnp.pad(b_f, (0, LANES - N_HEADS)).reshape(1, LANES)
    return pl.pallas_call(
        functools.partial(_fox_proj_kernel, tm=tm, cw=cw),
        grid=(s // tm,),
        in_specs=[_rows(tm, D), _resident((6, D)), _resident((1, D)),
                  _resident((D, 3 * D + LANES)), _resident((1, LANES)),
                  _resident((1, D)), _resident((1, D))],
        out_specs=[_rows(tm, D), _rows(tm, D), _cols(tm), _cols(tm), _rows(tm, D), _rows(tm, LANES)],
        out_shape=[jax.ShapeDtypeStruct((s, D), BF16)] * 2 + [jax.ShapeDtypeStruct((D, s), BF16)] * 2
        + [jax.ShapeDtypeStruct((s, D), BF16), jax.ShapeDtypeStruct((s, LANES), F32)],
        scratch_shapes=[pltpu.VMEM((SUBLANES, LANES), F32), pltpu.VMEM((tm, tm), BF16)],
        compiler_params=_params("arbitrary"),
        name="fox_proj",
    )(x, mod, g.reshape(1, D), w, bf, jnp.tile(qn, N_HEADS).reshape(1, D),
      jnp.tile(kn, N_HEADS).reshape(1, D))


def _moba_proj_kernel(x_ref, mod_ref, g_ref, w_ref, qn_ref, kn_ref,
                      q_ref, k_ref, v0_ref, v1_ref, qf_ref, km_ref, kx_ref, *, tm, cw):
    row = lax.broadcasted_iota(jnp.int32, (tm, LANES), 0)
    lane = lax.broadcasted_iota(jnp.int32, (tm, LANES), 1)
    rem = row & (MOBA_BLOCK - 1)
    block = pl.program_id(0) * (tm // MOBA_BLOCK) + lax.shift_right_logical(row, MOBA_BLOCK.bit_length() - 1)
    offset_lanes = jnp.where(lane < MAX_KEY_BLOCKS + 3, rem, row - rem).astype(F32)
    kx_ref[...] = jnp.where(
        lane < MAX_KEY_BLOCKS, jnp.where(lane == block, 1.0, 0.0),
        jnp.where(lane < MAX_KEY_BLOCKS + 6, offset_lanes, 0.0)).astype(BF16)

    h = _rms_mod(x_ref[...], g_ref[...], mod_ref[1:2, :], mod_ref[0:1, :]).astype(BF16)
    gm = _group_mean_matrix(cw)
    for c in range(D // cw):
        lo = c * cw
        q = _head_rms(_dot(h, w_ref[:, lo:lo + cw]), gm) * qn_ref[:, lo:lo + cw]
        qf_ref[:, lo:lo + cw] = q
        q_ref[:, lo:lo + cw] = (q * (SCALE * LOG2E)).astype(BF16)
        k = _head_rms(_dot(h, w_ref[:, D + lo:D + lo + cw]), gm) * kn_ref[:, lo:lo + cw]
        k_ref[:, lo:lo + cw] = k.astype(BF16)
        for b in range(tm // MOBA_BLOCK):
            km_ref[b, :, lo:lo + cw] = jnp.mean(
                k[b * MOBA_BLOCK:(b + 1) * MOBA_BLOCK, :], axis=0, keepdims=True)
        _store_v_with_ones(_dot(h, w_ref[:, 2 * D + lo:2 * D + lo + cw]), v0_ref, v1_ref, lo, cw)


def _moba_proj(x, mod, g, w_in, qn, kn, *, tm=ATTN_TILE, cw=256):
    s = x.shape[0]
    nb = tm // MOBA_BLOCK
    return pl.pallas_call(
        functools.partial(_moba_proj_kernel, tm=tm, cw=cw),
        grid=(s // tm,),
        in_specs=[_rows(tm, D), _resident((6, D)), _resident((1, D)), _resident((D, 3 * D)),
                  _resident((1, D)), _resident((1, D))],
        out_specs=[_rows(tm, D), _rows(tm, D), _cols(tm), _cols(tm), _rows(tm, D),
                   pl.BlockSpec((nb, 1, D), lambda i: (i, 0, 0)), _rows(tm, LANES)],
        out_shape=[jax.ShapeDtypeStruct((s, D), BF16)] * 2 + [jax.ShapeDtypeStruct((D, s), BF16)] * 2
        + [jax.ShapeDtypeStruct((s, D), F32), jax.ShapeDtypeStruct((s // MOBA_BLOCK, 1, D), F32),
           jax.ShapeDtypeStruct((s, LANES), BF16)],
        compiler_params=_params("parallel"),
        name="moba_proj",
    )(x, mod, g.reshape(1, D), w_in, jnp.tile(qn, N_HEADS).reshape(1, D),
      jnp.tile(kn, N_HEADS).reshape(1, D))


def _moba_gate_kernel(q_ref, km_ref, slope_ref, qx_ref):
    own = pl.program_id(0)
    t = q_ref.shape[0]
    blk = lax.broadcasted_iota(jnp.int32, (MAX_KEY_BLOCKS, t), 0)
    lane = lax.broadcasted_iota(jnp.int32, (t, LANES), 1)
    km_lane_head = lax.shift_right_logical(
        lax.broadcasted_iota(jnp.int32, (MAX_KEY_BLOCKS, MXU_W), 1), HEAD_DIM.bit_length() - 1)
    pad_rows = jnp.full((LANES - MAX_KEY_BLOCKS, t), NEG_INF, F32)
    for g4 in range(D // MXU_W):
        lo = g4 * MXU_W
        q_hi, q_lo = _split2(q_ref[:, lo:lo + MXU_W])
        km = km_ref[:, lo:lo + MXU_W]
        for hl in range(MXU_W // HEAD_DIM):
            head = 4 * g4 + hl
            km_hi, km_lo = _split2(jnp.where(km_lane_head == hl, km, 0.0))
            gate = _dot_nt(km_hi, q_hi) + _dot_nt(km_lo, q_hi) + _dot_nt(km_hi, q_lo)
            keep = blk == own
            gh = jnp.where(blk < own, gate, NEG_INF)
            for _ in range(MOBA_TOPK):
                mx = jnp.max(gh, axis=0, keepdims=True)
                idx = jnp.min(jnp.where(gh == mx, blk, MAX_KEY_BLOCKS), axis=0, keepdims=True)
                pick = blk == idx
                keep = keep | (pick & (idx < own))
                gh = jnp.where(pick, -jnp.inf, gh)
            bias = jnp.concatenate([jnp.where(keep, 0.0, NEG_INF), pad_rows], axis=0).T
            qx = jnp.where(lane < MAX_KEY_BLOCKS, bias, slope_ref[head:head + 1, :])
            qx_ref[:, head * LANES:(head + 1) * LANES] = qx.astype(BF16)


def _moba_gate(qf, kmean, slope_lanes):
    s = qf.shape[0]
    n_kb = s // MOBA_BLOCK
    assert n_kb <= MAX_KEY_BLOCKS
    km = jnp.pad(kmean.reshape(n_kb, D), ((0, MAX_KEY_BLOCKS - n_kb), (0, 0)))
    return pl.pallas_call(
        _moba_gate_kernel,
        grid=(n_kb,),
        in_specs=[_rows(MOBA_BLOCK, D), _resident((MAX_KEY_BLOCKS, D)), _resident((N_HEADS, LANES))],
        out_specs=_rows(MOBA_BLOCK, N_HEADS * LANES),
        out_shape=jax.ShapeDtypeStruct((s, N_HEADS * LANES), BF16),
        compiler_params=_params("parallel"),
        name="moba_gate",
    )(qf, km, slope_lanes)


def _attn_kernel(*refs, t, moba):
    if moba:
        nvis_ref, r_ref, q_ref, kk_ref, kx_ref, v0_ref, v1_ref, qx_ref, o_ref, acc_ref, m_ref, s_ref, mp_ref = refs
    else:
        nvis_ref, r_ref, q_ref, kk_ref, kx_ref, v0_ref, v1_ref, o_ref, acc_ref, m_ref, s_ref, mp_ref = refs
    pair = pl.program_id(0)
    i = pl.program_id(1)
    q = q_ref[...]
    lane = lax.broadcasted_iota(jnp.int32, (t, LANES), 1)
    row = lax.broadcasted_iota(jnp.int32, (t, t), 0)
    col = lax.broadcasted_iota(jnp.int32, (t, t), 1)
    v_refs = (v0_ref, v1_ref)
    n_past = [nvis_ref[2 * pair + hh, i] for hh in range(2)]

    def query(hh):
        mine = (lane < HEAD_DIM) if hh == 0 else (lane >= HEAD_DIM)
        if moba:
            qx = qx_ref[:, hh * LANES:(hh + 1) * LANES]
        else:
            qx = jnp.where((lane >= 3 * hh) & (lane < 3 * hh + 3), 1.0, 0.0).astype(BF16)
        return jnp.concatenate([jnp.where(mine, q, jnp.zeros_like(q)), qx], axis=1)

    qh = [query(hh) for hh in range(2)]

    def scores(hh, j):
        start = pl.multiple_of(j * t, t)
        kk = jnp.concatenate([kk_ref[pl.ds(start, t), :], kx_ref[pl.ds(start, t), :]], axis=1)
        return _dot_nt(kk, qh[hh])

    def values(hh, j):
        return v_refs[hh][:, pl.ds(pl.multiple_of(j * t, t), t)]

    def down(stat, rows):
        return jnp.concatenate([stat] * (rows // SUBLANES), axis=0)

    def produce(hh, j, slot):
        s = scores(hh, j)
        s_ref[hh, slot] = s
        part = s[0:SUBLANES, :]
        for g in range(1, t // SUBLANES):
            part = jnp.maximum(part, s[g * SUBLANES:(g + 1) * SUBLANES, :])
        mp_ref[hh, slot] = part

    def step(hh, j, cur, nxt=None):
        h = 2 * pair + hh
        if nxt is not None:
            produce(hh, jnp.maximum(j - 1, 0), nxt)
        s = s_ref[hh, cur]
        if moba:
            dist = jnp.zeros((1, 1), jnp.int32) + (i - j) * t
            c = -(r_ref[h, 0] * dist.astype(F32))
        else:
            c = (jnp.zeros((1, 1), F32) + r_ref[h, i]) - (jnp.zeros((1, 1), F32) + r_ref[h, j])
        m_old = m_ref[hh]
        m_tile = jnp.broadcast_to(jnp.max(mp_ref[hh, cur], axis=0, keepdims=True), (SUBLANES, t))
        m_new = jnp.maximum(m_old, m_tile + c)
        alpha = jnp.exp2(m_old - m_new)
        p = jnp.exp2(s - down(m_new - c, t))
        acc_ref[hh] = down(alpha, LANES) * acc_ref[hh] + _dot(values(hh, j), p.astype(BF16))
        m_ref[hh] = m_new

    for hh in range(2):
        s = jnp.where(row <= col, scores(hh, i), NEG_INF)
        m = jnp.broadcast_to(jnp.max(s, axis=0, keepdims=True), (SUBLANES, t))
        acc_ref[hh] = _dot(values(hh, i), jnp.exp2(s - down(m, t)).astype(BF16))
        m_ref[hh] = m
        produce(hh, jnp.maximum(i - 1, 0), 0)

    n_joint = jnp.minimum(n_past[0], n_past[1]) // 2

    def joint_body(k2, carry):
        j = i - 1 - 2 * k2
        for hh in range(2):
            step(hh, j, 0, 1)
        for hh in range(2):
            step(hh, j - 1, 1, 0)
        return carry

    lax.fori_loop(0, n_joint, joint_body, 0)

    outs = []
    for hh in range(2):
        def tail_body(k2, carry, hh=hh):
            j = i - 1 - 2 * k2
            step(hh, j, 0, 1)
            step(hh, j - 1, 1, 0)
            return carry

        n_pairs = n_past[hh] // 2
        lax.fori_loop(n_joint, n_pairs, tail_body, 0)

        @pl.when(n_past[hh] % 2 == 1)
        def _(hh=hh, n_pairs=n_pairs):
            step(hh, i - 1 - 2 * n_pairs, 0)

        acc = acc_ref[hh]
        own = acc[hh * HEAD_DIM:(hh + 1) * HEAD_DIM, :]
        sums = acc[(1 - hh) * HEAD_DIM:(2 - hh) * HEAD_DIM, :]
        outs.append(own * (1.0 / sums))
    o_ref[...] = jnp.concatenate(outs, axis=0).T.astype(BF16)


def _attention(nvis, r, q, kk, kx, v0, v1, qx=None, *, t):
    s = q.shape[0]
    moba = qx is not None
    col_block = pl.BlockSpec((s, LANES), lambda p, i, *_: (0, p), pipeline_mode=pl.Buffered(1))
    row_block = pl.BlockSpec((LANES, s), lambda p, i, *_: (p, 0), pipeline_mode=pl.Buffered(1))
    tile = pl.BlockSpec((t, LANES), lambda p, i, *_: (i, p))
    ins = [q, kk, kx, v0, v1]
    if moba:
        kx_spec = pl.BlockSpec((s, LANES), lambda p, i, *_: (0, 0), pipeline_mode=pl.Buffered(1))
        specs = [tile, col_block, kx_spec, row_block, row_block,
                 pl.BlockSpec((t, 2 * LANES), lambda p, i, *_: (i, p))]
        ins.append(qx)
    else:
        specs = [tile, col_block, col_block, row_block, row_block]
    return pl.pallas_call(
        functools.partial(_attn_kernel, t=t, moba=moba),
        grid_spec=pltpu.PrefetchScalarGridSpec(
            num_scalar_prefetch=2,
            grid=(N_PAIRS, s // t),
            in_specs=specs,
            out_specs=tile,
            scratch_shapes=[pltpu.VMEM((2, LANES, t), F32), pltpu.VMEM((2, SUBLANES, t), F32),
                            pltpu.VMEM((2, 2, t, t), F32), pltpu.VMEM((2, 2, SUBLANES, t), F32)],
        ),
        out_shape=jax.ShapeDtypeStruct((s, D), BF16),
        compiler_params=_params("parallel", "parallel"),
        name="moba_attn" if moba else "fox_attn",
    )(nvis, r, *ins)


def _first_visible(vis):
    nt = vis.shape[1]
    i = jnp.arange(nt)[None, :, None]
    j = jnp.arange(nt)[None, None, :]
    first = jnp.min(jnp.where(vis & (j < i), j, i), axis=2)
    return (i[:, :, 0] - first).astype(jnp.int32)


def _score_bound(qn, kn):
    return 1.02 * HEAD_DIM ** 0.5 * jnp.max(jnp.abs(qn)) * jnp.max(jnp.abs(kn))


def _fox_schedule(cum_f, qn, kn, t):
    f = cum_f[:, :N_HEADS].T
    f_before = jnp.concatenate([jnp.zeros((N_HEADS, 1), F32), f[:, t - 1:-1:t]], axis=1)
    reach = SKIP_THRESHOLD + 2.0 * _score_bound(qn, kn)
    vis = (f[:, 0::t, None] - f[:, None, t - 1::t]) >= -reach
    return _first_visible(vis), LOG2E * f_before


def _moba_constants(s, qn, kn, t):
    slopes = jnp.exp2(-ALIBI_MAX_EXP * jnp.arange(1, N_HEADS + 1, dtype=F32) / N_HEADS)
    hi, mid, lo = _split3(slopes * LOG2E)
    pieces = jnp.stack([hi, mid, lo, hi, mid, lo], axis=1).astype(F32)
    slope_lanes = jnp.zeros((N_HEADS, LANES), F32).at[:, MAX_KEY_BLOCKS:MAX_KEY_BLOCKS + 6].set(pieces)
    nt = s // t
    i = jnp.arange(nt)[None, :, None]
    j = jnp.arange(nt)[None, None, :]
    nearest = ((i - j - 1) * t + 1).astype(F32)
    reach = SKIP_THRESHOLD + 2.0 * _score_bound(qn, kn)
    nvis = _first_visible(slopes[:, None, None] * nearest <= reach)
    r = jnp.broadcast_to((slopes * LOG2E)[:, None], (N_HEADS, nt))
    return slope_lanes, nvis, r


def kernel(x, c, ln_mix_g, ln_ffn_g, ada_w, ada_b, ffn_up, ffn_conv, ffn_down, sc_in, sc_conv, sc_out, cf_in, cf_dw, cf_dw_b, cf_ln_g, cf_ln_b, cf_out, fox_in, fox_bf, fox_qn, fox_kn, fox_out, moba_in, moba_qn, moba_kn, moba_out):
    batch, s, _ = x.shape
    assert batch == 1 and c.shape[0] == 1
    depth = ada_w.shape[0]
    mods = _ada(c, ada_w, ada_b)
    ffn_up_bf16 = ffn_up.astype(BF16)
    ffn_down_bf16 = ffn_down.astype(BF16)
    xs = x.reshape(s, D)
    for i in range(depth):
        kind, j = i % 4, i // 4
        mod = mods[i]
        pre = None
        if kind == 0:
            xs = _sc_mixer(xs, mod, ln_mix_g[i], sc_in[j].astype(BF16), sc_conv[j],
                           sc_out[j].astype(BF16))
        elif kind == 1:
            xs = _cf_mixer(xs, mod, ln_mix_g[i], cf_in[j].astype(BF16), cf_dw[j], cf_dw_b[j],
                           cf_ln_g[j], cf_ln_b[j], cf_out[j].astype(BF16))
        elif kind == 2:
            q, k, v0, v1, kx, cum_f = _fox_proj(xs, mod, ln_mix_g[i], fox_in[j], fox_bf[j],
                                                fox_qn[j], fox_kn[j])
            nvis, r = _fox_schedule(cum_f, fox_qn[j], fox_kn[j], ATTN_TILE)
            pre = (_attention(nvis, r, q, k, kx, v0, v1, t=ATTN_TILE), fox_out[j].astype(BF16))
        else:
            q, k, v0, v1, qf, kmean, kx = _moba_proj(xs, mod, ln_mix_g[i], moba_in[j].astype(BF16),
                                                     moba_qn[j], moba_kn[j])
            slope_lanes, nvis, r = _moba_constants(s, moba_qn[j], moba_kn[j], ATTN_TILE)
            qx = _moba_gate(qf, kmean, slope_lanes)
            pre = (_attention(nvis, r, q, k, kx, v0, v1, qx, t=ATTN_TILE), moba_out[j].astype(BF16))
        xs = _ffn(xs, mod, ln_ffn_g[i], ffn_up_bf16, ffn_conv[i], ffn_down_bf16, i, pre)
    return xs.reshape(1, s, D)
```

```python
import functools

import jax
import jax.numpy as jnp
from jax import lax
from jax.experimental import pallas as pl
from jax.experimental.pallas import tpu as pltpu

D = 1024
N_HEADS = 16
HEAD_DIM = D // N_HEADS
N_PAIRS = N_HEADS // 2
D_FF = ((8 * D // 3 + 127) // 128) * 128
CONFORMER_K = 31
MOBA_BLOCK = 256
MOBA_TOPK = 3
ALIBI_MAX_EXP = 8.0
RMS_EPS = 1e-6
LN_EPS = 1e-5
NEG_INF = -1e30
SCALE = HEAD_DIM ** -0.5
LOG2E = 1.4426950408889634
SKIP_THRESHOLD = 106.0
ATTN_TILE = 512
MAX_KEY_BLOCKS = 64

LANES = 128
SUBLANES = 8
MXU_W = 256
VMEM_LIMIT = 56 * 1024 * 1024

F32 = jnp.float32
BF16 = jnp.bfloat16

_dot = functools.partial(jnp.dot, preferred_element_type=F32)


def _dot_nt(a, b):
    return lax.dot_general(a, b, (((1,), (1,)), ((), ())), preferred_element_type=F32)


def _resident(shape):
    nd = len(shape)
    return pl.BlockSpec(shape, lambda *_: (0,) * nd, pipeline_mode=pl.Buffered(1))


def _rows(tm, width):
    return pl.BlockSpec((tm, width), lambda i: (i, 0))


def _cols(tm):
    return pl.BlockSpec((D, tm), lambda i: (0, i))


def _params(*sem):
    return pltpu.CompilerParams(dimension_semantics=sem, vmem_limit_bytes=VMEM_LIMIT)


def _rms_mod(x, g, scale, shift):
    ms = jnp.mean(x * x, axis=-1, keepdims=True)
    return (x * lax.rsqrt(ms + RMS_EPS) * g) * (1.0 + scale) + shift


def _shifted(ext, s, halo, tm):
    a, b = divmod(s, SUBLANES)
    r = pltpu.roll(ext, b, axis=0) if b else ext
    lo = halo - a * SUBLANES
    return r[lo:lo + tm, :]


def _split2(z):
    hi = z.astype(BF16)
    lo = (z - hi.astype(F32)).astype(BF16)
    return hi, lo


def _split3(z):
    hi = z.astype(BF16)
    r = z - hi.astype(F32)
    mid = r.astype(BF16)
    lo = (r - mid.astype(F32)).astype(BF16)
    return hi, mid, lo


def _ada_kernel(c_ref, w_ref, b_ref, o_ref):
    cond = jax.nn.silu(c_ref[...])
    o_ref[0, 0] = jnp.sum(w_ref[0] * cond, axis=0, keepdims=True) + b_ref[0, 0]


def _ada(c, ada_w, ada_b):
    depth = ada_w.shape[0]
    out = pl.pallas_call(
        _ada_kernel,
        grid=(depth, 6),
        in_specs=[
            pl.BlockSpec((D, 1), lambda l, j: (0, 0)),
            pl.BlockSpec((1, D, D), lambda l, j: (l, 0, j)),
            pl.BlockSpec((1, 1, 1, D), lambda l, j: (l, j, 0, 0)),
        ],
        out_specs=pl.BlockSpec((1, 1, 1, D), lambda l, j: (l, j, 0, 0)),
        out_shape=jax.ShapeDtypeStruct((depth, 6, 1, D), F32),
        compiler_params=_params("parallel", "parallel"),
        name="ada",
    )(c.reshape(D, 1), ada_w, ada_b.reshape(depth, 6, 1, D))
    return out.reshape(depth, 6, D)


def _ffn_kernel(*refs, tm, cw, pre_proj):
    if pre_proj:
        (x_ref, a_ref, wo_ref, mod_ref, g_ref, wup_ref, wc_ref, wdn_ref, o_ref,
         carry_ref, act_ref) = refs
    else:
        x_ref, mod_ref, g_ref, wup_ref, wc_ref, wdn_ref, o_ref, carry_ref, act_ref = refs

    @pl.when(pl.program_id(0) == 0)
    def _():
        carry_ref[...] = jnp.zeros_like(carry_ref)

    x = x_ref[...]
    if pre_proj:
        x = x + mod_ref[2:3, :] * _dot(a_ref[...], wo_ref[...])
    h = _rms_mod(x, g_ref[...], mod_ref[4:5, :], mod_ref[3:4, :]).astype(BF16)
    for c in range(D_FF // cw):
        lo = c * cw
        u = _dot(h, wup_ref[:, lo:lo + cw])
        gate = _dot(h, wup_ref[:, D_FF + lo:D_FF + lo + cw])
        ext = jnp.concatenate([carry_ref[:, lo:lo + cw], u], axis=0)
        carry_ref[:, lo:lo + cw] = u[tm - SUBLANES:, :]
        conv = (wc_ref[0:1, lo:lo + cw] * _shifted(ext, 2, SUBLANES, tm)
                + wc_ref[1:2, lo:lo + cw] * _shifted(ext, 1, SUBLANES, tm)
                + wc_ref[2:3, lo:lo + cw] * u)
        act_ref[:, lo:lo + cw] = ((conv * jax.nn.sigmoid(conv)) * gate).astype(BF16)
    o_ref[...] = x + mod_ref[5:6, :] * _dot(act_ref[...], wdn_ref[...])


def _layer_resident(shape, layer):
    nd = len(shape)
    return pl.BlockSpec((None,) + tuple(shape), lambda *_: (layer,) + (0,) * nd,
                        pipeline_mode=pl.Buffered(1))


def _ffn(x, mod, g, w_up, w_conv, w_down, layer, pre=None, *, tm=512, cw=256):
    s = x.shape[0]
    ins, specs = [x], [_rows(tm, D)]
    if pre is not None:
        a, w_o = pre
        ins += [a, w_o]
        specs += [_rows(tm, D), _resident((D, D))]
    ins += [mod, g.reshape(1, D), w_up, w_conv, w_down]
    specs += [_resident((6, D)), _resident((1, D)), _layer_resident((D, 2 * D_FF), layer),
              _resident((3, D_FF)), _layer_resident((D_FF, D), layer)]
    return pl.pallas_call(
        functools.partial(_ffn_kernel, tm=tm, cw=cw, pre_proj=pre is not None),
        grid=(s // tm,),
        in_specs=specs,
        out_specs=_rows(tm, D),
        out_shape=jax.ShapeDtypeStruct((s, D), F32),
        scratch_shapes=[pltpu.VMEM((SUBLANES, D_FF), F32), pltpu.VMEM((tm, D_FF), BF16)],
        compiler_params=_params("arbitrary"),
        name="ffn_proj" if pre is not None else "ffn",
    )(*ins)


def _sc_kernel(x_ref, mod_ref, g_ref, win_ref, wc_ref, wout_ref, o_ref, carry_ref, act_ref,
               *, tm, cw):
    @pl.when(pl.program_id(0) == 0)
    def _():
        carry_ref[...] = jnp.zeros_like(carry_ref)

    x = x_ref[...]
    h = _rms_mod(x, g_ref[...], mod_ref[1:2, :], mod_ref[0:1, :]).astype(BF16)
    for c in range(D // cw):
        lo = c * cw
        b_gate = _dot(h, win_ref[:, lo:lo + cw])
        c_gate = _dot(h, win_ref[:, D + lo:D + lo + cw])
        xh = _dot(h, win_ref[:, 2 * D + lo:2 * D + lo + cw])
        u = c_gate * xh
        ext = jnp.concatenate([carry_ref[:, lo:lo + cw], u], axis=0)
        carry_ref[:, lo:lo + cw] = u[tm - SUBLANES:, :]
        conv = (wc_ref[0:1, lo:lo + cw] * _shifted(ext, 2, SUBLANES, tm)
                + wc_ref[1:2, lo:lo + cw] * _shifted(ext, 1, SUBLANES, tm)
                + wc_ref[2:3, lo:lo + cw] * u)
        act_ref[:, lo:lo + cw] = (b_gate * conv).astype(BF16)
    o_ref[...] = x + mod_ref[2:3, :] * _dot(act_ref[...], wout_ref[...])


def _sc_mixer(x, mod, g, w_in, w_conv, w_out, *, tm=512, cw=256):
    s = x.shape[0]
    return pl.pallas_call(
        functools.partial(_sc_kernel, tm=tm, cw=cw),
        grid=(s // tm,),
        in_specs=[_rows(tm, D), _resident((6, D)), _resident((1, D)), _resident((D, 3 * D)),
                  _resident((3, D)), _resident((D, D))],
        out_specs=_rows(tm, D),
        out_shape=jax.ShapeDtypeStruct((s, D), F32),
        scratch_shapes=[pltpu.VMEM((SUBLANES, D), F32), pltpu.VMEM((tm, D), BF16)],
        compiler_params=_params("arbitrary"),
        name="sc_mixer",
    )(x, mod, g.reshape(1, D), w_in, w_conv, w_out)


CF_HALO = 32


def _cf_kernel(x_ref, mod_ref, g_ref, win_ref, wdw_ref, bdw_ref, lng_ref, lnb_ref, wout_ref,
               o_ref, carry_ref, u_ref, *, tm, cw):
    @pl.when(pl.program_id(0) == 0)
    def _():
        carry_ref[...] = jnp.zeros_like(carry_ref)

    x = x_ref[...]
    h = _rms_mod(x, g_ref[...], mod_ref[1:2, :], mod_ref[0:1, :]).astype(BF16)
    for c in range(D // cw):
        lo = c * cw
        a = _dot(h, win_ref[:, lo:lo + cw])
        gate = _dot(h, win_ref[:, D + lo:D + lo + cw])
        glu = a * jax.nn.sigmoid(gate)
        ext = jnp.concatenate([carry_ref[:, lo:lo + cw], glu], axis=0)
        carry_ref[:, lo:lo + cw] = glu[tm - CF_HALO:, :]
        conv = jnp.zeros((tm, cw), F32) + bdw_ref[:, lo:lo + cw]
        for b in range(SUBLANES):
            r = pltpu.roll(ext, b, axis=0) if b else ext
            for a8 in range(CF_HALO // SUBLANES):
                s = a8 * SUBLANES + b
                if s < CONFORMER_K:
                    k = CONFORMER_K - 1 - s
                    start = CF_HALO - a8 * SUBLANES
                    conv = conv + wdw_ref[k:k + 1, lo:lo + cw] * r[start:start + tm, :]
        u_ref[:, lo:lo + cw] = conv
    u = u_ref[...]
    uc = u - jnp.mean(u, axis=-1, keepdims=True)
    y = uc * lax.rsqrt(jnp.mean(uc * uc, axis=-1, keepdims=True) + LN_EPS)
    y = y * lng_ref[...] + lnb_ref[...]
    y = (y * jax.nn.sigmoid(y)).astype(BF16)
    o_ref[...] = x + mod_ref[2:3, :] * _dot(y, wout_ref[...])


def _cf_mixer(x, mod, g, w_in, w_dw, b_dw, ln_g, ln_b, w_out, *, tm=512, cw=256):
    s = x.shape[0]
    return pl.pallas_call(
        functools.partial(_cf_kernel, tm=tm, cw=cw),
        grid=(s // tm,),
        in_specs=[_rows(tm, D), _resident((6, D)), _resident((1, D)), _resident((D, 2 * D)),
                  _resident((CONFORMER_K, D)), _resident((1, D)), _resident((1, D)),
                  _resident((1, D)), _resident((D, D))],
        out_specs=_rows(tm, D),
        out_shape=jax.ShapeDtypeStruct((s, D), F32),
        scratch_shapes=[pltpu.VMEM((CF_HALO, D), F32), pltpu.VMEM((tm, D), F32)],
        compiler_params=_params("arbitrary"),
        name="cf_mixer",
    )(x, mod, g.reshape(1, D), w_in, w_dw, b_dw.reshape(1, D), ln_g.reshape(1, D),
      ln_b.reshape(1, D), w_out)


def _group_mean_matrix(n):
    r = lax.broadcasted_iota(jnp.int32, (n, n), 0) // HEAD_DIM
    c = lax.broadcasted_iota(jnp.int32, (n, n), 1) // HEAD_DIM
    return jnp.where(r == c, 1.0 / HEAD_DIM, 0.0).astype(BF16)


def _head_rms(z, gm):
    hi, lo = _split2(z * z)
    ms = _dot(hi, gm) + _dot(lo, gm)
    return z * lax.rsqrt(ms + RMS_EPS)


def _store_v_with_ones(v, v0_ref, v1_ref, lo, cw):
    vt = v.T
    feature = lax.broadcasted_iota(jnp.int32, vt.shape, 0)
    first = (feature & (LANES - 1)) < HEAD_DIM
    v0_ref[lo:lo + cw, :] = jnp.where(first, vt, 1.0).astype(BF16)
    v1_ref[lo:lo + cw, :] = jnp.where(first, 1.0, vt).astype(BF16)


def _fox_proj_kernel(x_ref, mod_ref, g_ref, w_ref, bf_ref, qn_ref, kn_ref,
                     q_ref, k_ref, v0_ref, v1_ref, kx_ref, f_ref, carry_ref, tri_ref, *, tm, cw):
    @pl.when(pl.program_id(0) == 0)
    def _():
        carry_ref[...] = jnp.zeros_like(carry_ref)
        r = lax.broadcasted_iota(jnp.int32, (tm, tm), 0)
        c = lax.broadcasted_iota(jnp.int32, (tm, tm), 1)
        tri_ref[...] = jnp.where(r >= c, 1.0, 0.0).astype(BF16)

    h = _rms_mod(x_ref[...], g_ref[...], mod_ref[1:2, :], mod_ref[0:1, :]).astype(BF16)
    gm = _group_mean_matrix(cw)
    for c in range(D // cw):
        lo = c * cw
        q = _head_rms(_dot(h, w_ref[:, lo:lo + cw]), gm)
        q_ref[:, lo:lo + cw] = (q * qn_ref[:, lo:lo + cw] * (SCALE * LOG2E)).astype(BF16)
        k = _head_rms(_dot(h, w_ref[:, D + lo:D + lo + cw]), gm)
        k_ref[:, lo:lo + cw] = (k * kn_ref[:, lo:lo + cw]).astype(BF16)
        _store_v_with_ones(_dot(h, w_ref[:, 2 * D + lo:2 * D + lo + cw]), v0_ref, v1_ref, lo, cw)
    fl = _dot(h, w_ref[:, 3 * D:3 * D + LANES]) + bf_ref[...]
    lf = jnp.minimum(fl, 0.0) - jnp.log1p(jnp.exp(-jnp.abs(fl)))
    tri = tri_ref[...]
    hi, mid, lo3 = _split3(lf)
    local = _dot(tri, hi) + _dot(tri, mid) + _dot(tri, lo3)
    cum = local + carry_ref[0:1, :]
    f_ref[...] = cum
    carry_ref[...] = jnp.broadcast_to(cum[tm - 1:tm, :], carry_ref.shape)
    head = lax.broadcasted_iota(jnp.int32, (LANES, D), 0)
    lane = lax.broadcasted_iota(jnp.int32, (LANES, D), 1)
    target = lax.shift_right_logical(head, 1) * LANES + (head & 1) * 3
    kx = jnp.zeros((tm, D), F32)
    for t, piece in enumerate(_split3(local * (-LOG2E))):
        place = jnp.where((lane == target + t) & (head < N_HEADS), 1.0, 0.0).astype(BF16)
        kx = kx + _dot(piece, place)
    kx_ref[...] = kx.astype(BF16)


def _fox_proj(x, mod, g, w_in, b_f, qn, kn, *, tm=ATTN_TILE, cw=256):
    s = x.shape[0]
    w = jnp.pad(w_in, ((0, 0), (0, LANES - N_HEADS))).astype(BF16)
    bf = jnp.pad(b_f, (0, LANES - N_HEADS)).reshape(1, LANES)
    return pl.pallas_call(
        functools.partial(_fox_proj_kernel, tm=tm, cw=cw),
        grid=(s // tm,),
        in_specs=[_rows(tm, D), _resident((6, D)), _resident((1, D)),
                  _resident((D, 3 * D + LANES)), _resident((1, LANES)),
                  _resident((1, D)), _resident((1, D))],
        out_specs=[_rows(tm, D), _rows(tm, D), _cols(tm), _cols(tm), _rows(tm, D), _rows(tm, LANES)],
        out_shape=[jax.ShapeDtypeStruct((s, D), BF16)] * 2 + [jax.ShapeDtypeStruct((D, s), BF16)] * 2
        + [jax.ShapeDtypeStruct((s, D), BF16), jax.ShapeDtypeStruct((s, LANES), F32)],
        scratch_shapes=[pltpu.VMEM((SUBLANES, LANES), F32), pltpu.VMEM((tm, tm), BF16)],
        compiler_params=_params("arbitrary"),
        name="fox_proj",
    )(x, mod, g.reshape(1, D), w, bf, jnp.tile(qn, N_HEADS).reshape(1, D),
      jnp.tile(kn, N_HEADS).reshape(1, D))


def _moba_proj_kernel(x_ref, mod_ref, g_ref, w_ref, qn_ref, kn_ref,
                      q_ref, k_ref, v0_ref, v1_ref, qf_ref, km_ref, kx_ref, *, tm, cw):
    row = lax.broadcasted_iota(jnp.int32, (tm, LANES), 0)
    lane = lax.broadcasted_iota(jnp.int32, (tm, LANES), 1)
    rem = row & (MOBA_BLOCK - 1)
    block = pl.program_id(0) * (tm // MOBA_BLOCK) + lax.shift_right_logical(row, MOBA_BLOCK.bit_length() - 1)
    offset_lanes = jnp.where(lane < MAX_KEY_BLOCKS + 3, rem, row - rem).astype(F32)
    kx_ref[...] = jnp.where(
        lane < MAX_KEY_BLOCKS, jnp.where(lane == block, 1.0, 0.0),
        jnp.where(lane < MAX_KEY_BLOCKS + 6, offset_lanes, 0.0)).astype(BF16)

    h = _rms_mod(x_ref[...], g_ref[...], mod_ref[1:2, :], mod_ref[0:1, :]).astype(BF16)
    gm = _group_mean_matrix(cw)
    for c in range(D // cw):
        lo = c * cw
        q = _head_rms(_dot(h, w_ref[:, lo:lo + cw]), gm) * qn_ref[:, lo:lo + cw]
        qf_ref[:, lo:lo + cw] = q
        q_ref[:, lo:lo + cw] = (q * (SCALE * LOG2E)).astype(BF16)
        k = _head_rms(_dot(h, w_ref[:, D + lo:D + lo + cw]), gm) * kn_ref[:, lo:lo + cw]
        k_ref[:, lo:lo + cw] = k.astype(BF16)
        for b in range(tm // MOBA_BLOCK):
            km_ref[b, :, lo:lo + cw] = jnp.mean(
                k[b * MOBA_BLOCK:(b + 1) * MOBA_BLOCK, :], axis=0, keepdims=True)
        _store_v_with_ones(_dot(h, w_ref[:, 2 * D + lo:2 * D + lo + cw]), v0_ref, v1_ref, lo, cw)


def _moba_proj(x, mod, g, w_in, qn, kn, *, tm=ATTN_TILE, cw=256):
    s = x.shape[0]
    nb = tm // MOBA_BLOCK
    return pl.pallas_call(
        functools.partial(_moba_proj_kernel, tm=tm, cw=cw),
        grid=(s // tm,),
        in_specs=[_rows(tm, D), _resident((6, D)), _resident((1, D)), _resident((D, 3 * D)),
                  _resident((1, D)), _resident((1, D))],
        out_specs=[_rows(tm, D), _rows(tm, D), _cols(tm), _cols(tm), _rows(tm, D),
                   pl.BlockSpec((nb, 1, D), lambda i: (i, 0, 0)), _rows(tm, LANES)],
        out_shape=[jax.ShapeDtypeStruct((s, D), BF16)] * 2 + [jax.ShapeDtypeStruct((D, s), BF16)] * 2
        + [jax.ShapeDtypeStruct((s, D), F32), jax.ShapeDtypeStruct((s // MOBA_BLOCK, 1, D), F32),
           jax.ShapeDtypeStruct((s, LANES), BF16)],
        compiler_params=_params("parallel"),
        name="moba_proj",
    )(x, mod, g.reshape(1, D), w_in, jnp.tile(qn, N_HEADS).reshape(1, D),
      jnp.tile(kn, N_HEADS).reshape(1, D))


def _moba_gate_kernel(q_ref, km_ref, slope_ref, qx_ref):
    own = pl.program_id(0)
    t = q_ref.shape[0]
    blk = lax.broadcasted_iota(jnp.int32, (MAX_KEY_BLOCKS, t), 0)
    lane = lax.broadcasted_iota(jnp.int32, (t, LANES), 1)
    km_lane_head = lax.shift_right_logical(
        lax.broadcasted_iota(jnp.int32, (MAX_KEY_BLOCKS, MXU_W), 1), HEAD_DIM.bit_length() - 1)
    pad_rows = jnp.full((LANES - MAX_KEY_BLOCKS, t), NEG_INF, F32)
    for g4 in range(D // MXU_W):
        lo = g4 * MXU_W
        q_hi, q_lo = _split2(q_ref[:, lo:lo + MXU_W])
        km = km_ref[:, lo:lo + MXU_W]
        for hl in range(MXU_W // HEAD_DIM):
            head = 4 * g4 + hl
            km_hi, km_lo = _split2(jnp.where(km_lane_head == hl, km, 0.0))
            gate = _dot_nt(km_hi, q_hi) + _dot_nt(km_lo, q_hi) + _dot_nt(km_hi, q_lo)
            keep = blk == own
            gh = jnp.where(blk < own, gate, NEG_INF)
            for _ in range(MOBA_TOPK):
                mx = jnp.max(gh, axis=0, keepdims=True)
                idx = jnp.min(jnp.where(gh == mx, blk, MAX_KEY_BLOCKS), axis=0, keepdims=True)
                pick = blk == idx
                keep = keep | (pick & (idx < own))
                gh = jnp.where(pick, -jnp.inf, gh)
            bias = jnp.concatenate([jnp.where(keep, 0.0, NEG_INF), pad_rows], axis=0).T
            qx = jnp.where(lane < MAX_KEY_BLOCKS, bias, slope_ref[head:head + 1, :])
            qx_ref[:, head * LANES:(head + 1) * LANES] = qx.astype(BF16)


def _moba_gate(qf, kmean, slope_lanes):
    s = qf.shape[0]
    n_kb = s // MOBA_BLOCK
    assert n_kb <= MAX_KEY_BLOCKS
    km = jnp.pad(kmean.reshape(n_kb, D), ((0, MAX_KEY_BLOCKS - n_kb), (0, 0)))
    return pl.pallas_call(
        _moba_gate_kernel,
        grid=(n_kb,),
        in_specs=[_rows(MOBA_BLOCK, D), _resident((MAX_KEY_BLOCKS, D)), _resident((N_HEADS, LANES))],
        out_specs=_rows(MOBA_BLOCK, N_HEADS * LANES),
        out_shape=jax.ShapeDtypeStruct((s, N_HEADS * LANES), BF16),
        compiler_params=_params("parallel"),
        name="moba_gate",
    )(qf, km, slope_lanes)


def _attn_kernel(*refs, t, moba):
    if moba:
        nvis_ref, r_ref, q_ref, kk_ref, kx_ref, v0_ref, v1_ref, qx_ref, o_ref, acc_ref, m_ref, s_ref, mp_ref = refs
    else:
        nvis_ref, r_ref, q_ref, kk_ref, kx_ref, v0_ref, v1_ref, o_ref, acc_ref, m_ref, s_ref, mp_ref = refs
    pair = pl.program_id(0)
    i = pl.program_id(1)
    q = q_ref[...]
    lane = lax.broadcasted_iota(jnp.int32, (t, LANES), 1)
    row = lax.broadcasted_iota(jnp.int32, (t, t), 0)
    col = lax.broadcasted_iota(jnp.int32, (t, t), 1)
    v_refs = (v0_ref, v1_ref)
    n_past = [nvis_ref[2 * pair + hh, i] for hh in range(2)]

    def query(hh):
        mine = (lane < HEAD_DIM) if hh == 0 else (lane >= HEAD_DIM)
        if moba:
            qx = qx_ref[:, hh * LANES:(hh + 1) * LANES]
        else:
            qx = jnp.where((lane >= 3 * hh) & (lane < 3 * hh + 3), 1.0, 0.0).astype(BF16)
        return jnp.concatenate([jnp.where(mine, q, jnp.zeros_like(q)), qx], axis=1)

    qh = [query(hh) for hh in range(2)]

    def scores(hh, j):
        start = pl.multiple_of(j * t, t)
        kk = jnp.concatenate([kk_ref[pl.ds(start, t), :], kx_ref[pl.ds(start, t), :]], axis=1)
        return _dot_nt(kk, qh[hh])

    def values(hh, j):
        return v_refs[hh][:, pl.ds(pl.multiple_of(j * t, t), t)]

    def down(stat, rows):
        return jnp.concatenate([stat] * (rows // SUBLANES), axis=0)

    def produce(hh, j, slot, causal=False):
        s = scores(hh, j)
        if causal:
            s = jnp.where(row <= col, s, NEG_INF)
        s_ref[hh, slot] = s
        part = s[0:SUBLANES, :]
        for g in range(1, t // SUBLANES):
            part = jnp.maximum(part, s[g * SUBLANES:(g + 1) * SUBLANES, :])
        mp_ref[hh, slot] = part

    def step(hh, j, cur, nxt=None):
        h = 2 * pair + hh
        if nxt is not None:
            produce(hh, jnp.maximum(j - 1, 0), nxt)
        s = s_ref[hh, cur]
        if moba:
            dist = jnp.zeros((1, 1), jnp.int32) + (i - j) * t
            c = -(r_ref[h, 0] * dist.astype(F32))
        else:
            c = (jnp.zeros((1, 1), F32) + r_ref[h, i]) - (jnp.zeros((1, 1), F32) + r_ref[h, j])
        m_old = m_ref[hh]
        m_tile = jnp.broadcast_to(jnp.max(mp_ref[hh, cur], axis=0, keepdims=True), (SUBLANES, t))
        m_new = jnp.maximum(m_old, m_tile + c)
        alpha = jnp.exp2(m_old - m_new)
        p = jnp.exp2(s - down(m_new - c, t))
        acc_ref[hh] = down(alpha, LANES) * acc_ref[hh] + _dot(values(hh, j), p.astype(BF16))
        m_ref[hh] = m_new

    for hh in range(2):
        acc_ref[hh] = jnp.zeros((LANES, t), F32)
        m_ref[hh] = jnp.full((SUBLANES, t), NEG_INF, F32)
        produce(hh, i, 0, causal=True)
    n_tiles = [n_past[hh] + 1 for hh in range(2)]

    n_joint = jnp.minimum(n_tiles[0], n_tiles[1]) // 2

    def joint_body(k2, carry):
        j = i - 2 * k2
        for hh in range(2):
            step(hh, j, 0, 1)
        for hh in range(2):
            step(hh, j - 1, 1, 0)
        return carry

    lax.fori_loop(0, n_joint, joint_body, 0)

    outs = []
    for hh in range(2):
        def tail_body(k2, carry, hh=hh):
            j = i - 2 * k2
            step(hh, j, 0, 1)
            step(hh, j - 1, 1, 0)
            return carry

        n_pairs = n_tiles[hh] // 2
        lax.fori_loop(n_joint, n_pairs, tail_body, 0)

        @pl.when(n_tiles[hh] % 2 == 1)
        def _(hh=hh, n_pairs=n_pairs):
            step(hh, i - 2 * n_pairs, 0)

        acc = acc_ref[hh]
        own = acc[hh * HEAD_DIM:(hh + 1) * HEAD_DIM, :]
        sums = acc[(1 - hh) * HEAD_DIM:(2 - hh) * HEAD_DIM, :]
        outs.append(own * (1.0 / sums))
    o_ref[...] = jnp.concatenate(outs, axis=0).T.astype(BF16)


def _attention(nvis, r, q, kk, kx, v0, v1, qx=None, *, t):
    s = q.shape[0]
    moba = qx is not None
    col_block = pl.BlockSpec((s, LANES), lambda p, i, *_: (0, p))
    row_block = pl.BlockSpec((LANES, s), lambda p, i, *_: (p, 0))
    tile = pl.BlockSpec((t, LANES), lambda p, i, *_: (i, p))
    ins = [q, kk, kx, v0, v1]
    if moba:
        kx_spec = pl.BlockSpec((s, LANES), lambda p, i, *_: (0, 0), pipeline_mode=pl.Buffered(1))
        specs = [tile, col_block, kx_spec, row_block, row_block,
                 pl.BlockSpec((t, 2 * LANES), lambda p, i, *_: (i, p))]
        ins.append(qx)
    else:
        specs = [tile, col_block, col_block, row_block, row_block]
    return pl.pallas_call(
        functools.partial(_attn_kernel, t=t, moba=moba),
        grid_spec=pltpu.PrefetchScalarGridSpec(
            num_scalar_prefetch=2,
            grid=(N_PAIRS, s // t),
            in_specs=specs,
            out_specs=tile,
            scratch_shapes=[pltpu.VMEM((2, LANES, t), F32), pltpu.VMEM((2, SUBLANES, t), F32),
                            pltpu.VMEM((2, 2, t, t), F32), pltpu.VMEM((2, 2, SUBLANES, t), F32)],
        ),
        out_shape=jax.ShapeDtypeStruct((s, D), BF16),
        compiler_params=_params("parallel", "parallel"),
        name="moba_attn" if moba else "fox_attn",
    )(nvis, r, *ins)


def _first_visible(vis):
    nt = vis.shape[1]
    i = jnp.arange(nt)[None, :, None]
    j = jnp.arange(nt)[None, None, :]
    first = jnp.min(jnp.where(vis & (j < i), j, i), axis=2)
    return (i[:, :, 0] - first).astype(jnp.int32)


def _score_bound(qn, kn):
    return 1.02 * HEAD_DIM ** 0.5 * jnp.max(jnp.abs(qn)) * jnp.max(jnp.abs(kn))


def _fox_schedule(cum_f, qn, kn, t):
    f = cum_f[:, :N_HEADS].T
    f_before = jnp.concatenate([jnp.zeros((N_HEADS, 1), F32), f[:, t - 1:-1:t]], axis=1)
    reach = SKIP_THRESHOLD + 2.0 * _score_bound(qn, kn)
    vis = (f[:, 0::t, None] - f[:, None, t - 1::t]) >= -reach
    return _first_visible(vis), LOG2E * f_before


def _moba_constants(s, qn, kn, t):
    slopes = jnp.exp2(-ALIBI_MAX_EXP * jnp.arange(1, N_HEADS + 1, dtype=F32) / N_HEADS)
    hi, mid, lo = _split3(slopes * LOG2E)
    pieces = jnp.stack([hi, mid, lo, hi, mid, lo], axis=1).astype(F32)
    slope_lanes = jnp.zeros((N_HEADS, LANES), F32).at[:, MAX_KEY_BLOCKS:MAX_KEY_BLOCKS + 6].set(pieces)
    nt = s // t
    i = jnp.arange(nt)[None, :, None]
    j = jnp.arange(nt)[None, None, :]
    nearest = ((i - j - 1) * t + 1).astype(F32)
    reach = SKIP_THRESHOLD + 2.0 * _score_bound(qn, kn)
    nvis = _first_visible(slopes[:, None, None] * nearest <= reach)
    r = jnp.broadcast_to((slopes * LOG2E)[:, None], (N_HEADS, nt))
    return slope_lanes, nvis, r


def kernel(x, c, ln_mix_g, ln_ffn_g, ada_w, ada_b, ffn_up, ffn_conv, ffn_down, sc_in, sc_conv, sc_out, cf_in, cf_dw, cf_dw_b, cf_ln_g, cf_ln_b, cf_out, fox_in, fox_bf, fox_qn, fox_kn, fox_out, moba_in, moba_qn, moba_kn, moba_out):
    batch, s, _ = x.shape
    assert batch == 1 and c.shape[0] == 1
    depth = ada_w.shape[0]
    mods = _ada(c, ada_w, ada_b)
    ffn_up_bf16 = ffn_up.astype(BF16)
    ffn_down_bf16 = ffn_down.astype(BF16)
    xs = x.reshape(s, D)
    for i in range(depth):
        kind, j = i % 4, i // 4
        mod = mods[i]
        pre = None
        if kind == 0:
            xs = _sc_mixer(xs, mod, ln_mix_g[i], sc_in[j].astype(BF16), sc_conv[j],
                           sc_out[j].astype(BF16))
        elif kind == 1:
            xs = _cf_mixer(xs, mod, ln_mix_g[i], cf_in[j].astype(BF16), cf_dw[j], cf_dw_b[j],
                           cf_ln_g[j], cf_ln_b[j], cf_out[j].astype(BF16))
        elif kind == 2:
            q, k, v0, v1, kx, cum_f = _fox_proj(xs, mod, ln_mix_g[i], fox_in[j], fox_bf[j],
                                                fox_qn[j], fox_kn[j])
            nvis, r = _fox_schedule(cum_f, fox_qn[j], fox_kn[j], ATTN_TILE)
            pre = (_attention(nvis, r, q, k, kx, v0, v1, t=ATTN_TILE), fox_out[j].astype(BF16))
        else:
            q, k, v0, v1, qf, kmean, kx = _moba_proj(xs, mod, ln_mix_g[i], moba_in[j].astype(BF16),
                                                     moba_qn[j], moba_kn[j])
            slope_lanes, nvis, r = _moba_constants(s, moba_qn[j], moba_kn[j], ATTN_TILE)
            qx = _moba_gate(qf, kmean, slope_lanes)
            pre = (_attention(nvis, r, q, k, kx, v0, v1, qx, t=ATTN_TILE), moba_out[j].astype(BF16))
        xs = _ffn(xs, mod, ln_ffn_g[i], ffn_up_bf16, ffn_conv[i], ffn_down_bf16, i, pre)
    return xs.reshape(1, s, D)
```

```python
import functools

import jax
import jax.numpy as jnp
from jax import lax
from jax.experimental import pallas as pl
from jax.experimental.pallas import tpu as pltpu

D = 1024
N_HEADS = 16
HEAD_DIM = D // N_HEADS
N_PAIRS = N_HEADS // 2
D_FF = ((8 * D // 3 + 127) // 128) * 128
CONFORMER_K = 31
MOBA_BLOCK = 256
MOBA_TOPK = 3
ALIBI_MAX_EXP = 8.0
RMS_EPS = 1e-6
LN_EPS = 1e-5
NEG_INF = -1e30
SCALE = HEAD_DIM ** -0.5
LOG2E = 1.4426950408889634
SKIP_THRESHOLD = 106.0
ATTN_TILE = 512
MAX_KEY_BLOCKS = 64

LANES = 128
SUBLANES = 8
MXU_W = 256
VMEM_LIMIT = 56 * 1024 * 1024

F32 = jnp.float32
BF16 = jnp.bfloat16

_dot = functools.partial(jnp.dot, preferred_element_type=F32)


def _dot_nt(a, b):
    return lax.dot_general(a, b, (((1,), (1,)), ((), ())), preferred_element_type=F32)


def _resident(shape):
    nd = len(shape)
    return pl.BlockSpec(shape, lambda *_: (0,) * nd, pipeline_mode=pl.Buffered(1))


def _rows(tm, width):
    return pl.BlockSpec((tm, width), lambda i: (i, 0))


def _cols(tm):
    return pl.BlockSpec((D, tm), lambda i: (0, i))


def _params(*sem):
    return pltpu.CompilerParams(dimension_semantics=sem, vmem_limit_bytes=VMEM_LIMIT)


def _rms_mod(x, g, scale, shift):
    ms = jnp.mean(x * x, axis=-1, keepdims=True)
    return (x * lax.rsqrt(ms + RMS_EPS) * g) * (1.0 + scale) + shift


def _shifted(ext, s, halo, tm):
    a, b = divmod(s, SUBLANES)
    r = pltpu.roll(ext, b, axis=0) if b else ext
    lo = halo - a * SUBLANES
    return r[lo:lo + tm, :]


def _split2(z):
    hi = z.astype(BF16)
    lo = (z - hi.astype(F32)).astype(BF16)
    return hi, lo


def _split3(z):
    hi = z.astype(BF16)
    r = z - hi.astype(F32)
    mid = r.astype(BF16)
    lo = (r - mid.astype(F32)).astype(BF16)
    return hi, mid, lo


def _ada_kernel(c_ref, w_ref, b_ref, o_ref):
    cond = jax.nn.silu(c_ref[...])
    o_ref[0, 0] = jnp.sum(w_ref[0] * cond, axis=0, keepdims=True) + b_ref[0, 0]


def _ada(c, ada_w, ada_b):
    depth = ada_w.shape[0]
    out = pl.pallas_call(
        _ada_kernel,
        grid=(depth, 6),
        in_specs=[
            pl.BlockSpec((D, 1), lambda l, j: (0, 0)),
            pl.BlockSpec((1, D, D), lambda l, j: (l, 0, j)),
            pl.BlockSpec((1, 1, 1, D), lambda l, j: (l, j, 0, 0)),
        ],
        out_specs=pl.BlockSpec((1, 1, 1, D), lambda l, j: (l, j, 0, 0)),
        out_shape=jax.ShapeDtypeStruct((depth, 6, 1, D), F32),
        compiler_params=_params("parallel", "parallel"),
        name="ada",
    )(c.reshape(D, 1), ada_w, ada_b.reshape(depth, 6, 1, D))
    return out.reshape(depth, 6, D)


def _ffn_body(x, mod_ref, g_ref, wup_ref, wc_ref, wdn_ref, carry_ref, act_ref, tm, cw):
    h = _rms_mod(x, g_ref[...], mod_ref[4:5, :], mod_ref[3:4, :]).astype(BF16)
    for c in range(D_FF // cw):
        lo = c * cw
        u = _dot(h, wup_ref[:, lo:lo + cw])
        gate = _dot(h, wup_ref[:, D_FF + lo:D_FF + lo + cw])
        ext = jnp.concatenate([carry_ref[:, lo:lo + cw], u], axis=0)
        carry_ref[:, lo:lo + cw] = u[tm - SUBLANES:, :]
        conv = (wc_ref[0:1, lo:lo + cw] * _shifted(ext, 2, SUBLANES, tm)
                + wc_ref[1:2, lo:lo + cw] * _shifted(ext, 1, SUBLANES, tm)
                + wc_ref[2:3, lo:lo + cw] * u)
        act_ref[:, lo:lo + cw] = ((conv * jax.nn.sigmoid(conv)) * gate).astype(BF16)
    return x + mod_ref[5:6, :] * _dot(act_ref[...], wdn_ref[...])


def _ffn_specs(layer):
    return [_resident((1, D)), _layer_resident((D, 2 * D_FF), layer), _resident((3, D_FF)),
            _layer_resident((D_FF, D), layer)]


def _ffn_scratch(tm):
    return [pltpu.VMEM((SUBLANES, D_FF), F32), pltpu.VMEM((tm, D_FF), BF16)]


def _ffn_kernel(*refs, tm, cw, pre_proj):
    if pre_proj:
        (x_ref, a_ref, wo_ref, mod_ref, g_ref, wup_ref, wc_ref, wdn_ref, o_ref,
         carry_ref, act_ref) = refs
    else:
        x_ref, mod_ref, g_ref, wup_ref, wc_ref, wdn_ref, o_ref, carry_ref, act_ref = refs

    @pl.when(pl.program_id(0) == 0)
    def _():
        carry_ref[...] = jnp.zeros_like(carry_ref)

    x = x_ref[...]
    if pre_proj:
        x = x + mod_ref[2:3, :] * _dot(a_ref[...], wo_ref[...])
    o_ref[...] = _ffn_body(x, mod_ref, g_ref, wup_ref, wc_ref, wdn_ref, carry_ref, act_ref, tm, cw)


def _layer_resident(shape, layer):
    nd = len(shape)
    return pl.BlockSpec((None,) + tuple(shape), lambda *_: (layer,) + (0,) * nd,
                        pipeline_mode=pl.Buffered(1))


def _ffn(x, mod, g, w_up, w_conv, w_down, layer, pre=None, *, tm=512, cw=256):
    s = x.shape[0]
    ins, specs = [x], [_rows(tm, D)]
    if pre is not None:
        a, w_o = pre
        ins += [a, w_o]
        specs += [_rows(tm, D), _resident((D, D))]
    ins += [mod, g.reshape(1, D), w_up, w_conv, w_down]
    specs += [_resident((6, D)), _resident((1, D)), _layer_resident((D, 2 * D_FF), layer),
              _resident((3, D_FF)), _layer_resident((D_FF, D), layer)]
    return pl.pallas_call(
        functools.partial(_ffn_kernel, tm=tm, cw=cw, pre_proj=pre is not None),
        grid=(s // tm,),
        in_specs=specs,
        out_specs=_rows(tm, D),
        out_shape=jax.ShapeDtypeStruct((s, D), F32),
        scratch_shapes=[pltpu.VMEM((SUBLANES, D_FF), F32), pltpu.VMEM((tm, D_FF), BF16)],
        compiler_params=_params("arbitrary"),
        name="ffn_proj" if pre is not None else "ffn",
    )(*ins)


def _sc_kernel(x_ref, mod_ref, g_ref, win_ref, wc_ref, wout_ref, g2_ref, wup_ref, wc2_ref, wdn_ref,
               o_ref, carry_ref, act_ref, carry2_ref, act2_ref, *, tm, cw):
    @pl.when(pl.program_id(0) == 0)
    def _():
        carry_ref[...] = jnp.zeros_like(carry_ref)
        carry2_ref[...] = jnp.zeros_like(carry2_ref)

    x = x_ref[...]
    h = _rms_mod(x, g_ref[...], mod_ref[1:2, :], mod_ref[0:1, :]).astype(BF16)
    for c in range(D // cw):
        lo = c * cw
        b_gate = _dot(h, win_ref[:, lo:lo + cw])
        c_gate = _dot(h, win_ref[:, D + lo:D + lo + cw])
        xh = _dot(h, win_ref[:, 2 * D + lo:2 * D + lo + cw])
        u = c_gate * xh
        ext = jnp.concatenate([carry_ref[:, lo:lo + cw], u], axis=0)
        carry_ref[:, lo:lo + cw] = u[tm - SUBLANES:, :]
        conv = (wc_ref[0:1, lo:lo + cw] * _shifted(ext, 2, SUBLANES, tm)
                + wc_ref[1:2, lo:lo + cw] * _shifted(ext, 1, SUBLANES, tm)
                + wc_ref[2:3, lo:lo + cw] * u)
        act_ref[:, lo:lo + cw] = (b_gate * conv).astype(BF16)
    x = x + mod_ref[2:3, :] * _dot(act_ref[...], wout_ref[...])
    o_ref[...] = _ffn_body(x, mod_ref, g2_ref, wup_ref, wc2_ref, wdn_ref, carry2_ref, act2_ref, tm, cw)


def _sc_layer(x, mod, g, w_in, w_conv, w_out, ffn, layer, *, tm=512, cw=256):
    s = x.shape[0]
    g2, w_up, w_conv2, w_down = ffn
    return pl.pallas_call(
        functools.partial(_sc_kernel, tm=tm, cw=cw),
        grid=(s // tm,),
        in_specs=[_rows(tm, D), _resident((6, D)), _resident((1, D)), _resident((D, 3 * D)),
                  _resident((3, D)), _resident((D, D))] + _ffn_specs(layer),
        out_specs=_rows(tm, D),
        out_shape=jax.ShapeDtypeStruct((s, D), F32),
        scratch_shapes=[pltpu.VMEM((SUBLANES, D), F32), pltpu.VMEM((tm, D), BF16)] + _ffn_scratch(tm),
        compiler_params=_params("arbitrary"),
        name="sc_layer",
    )(x, mod, g.reshape(1, D), w_in, w_conv, w_out, g2.reshape(1, D), w_up, w_conv2, w_down)


CF_HALO = 32


def _cf_kernel(x_ref, mod_ref, g_ref, win_ref, wdw_ref, bdw_ref, lng_ref, lnb_ref, wout_ref,
               g2_ref, wup_ref, wc2_ref, wdn_ref, o_ref, carry_ref, u_ref, carry2_ref, act2_ref,
               *, tm, cw):
    @pl.when(pl.program_id(0) == 0)
    def _():
        carry_ref[...] = jnp.zeros_like(carry_ref)
        carry2_ref[...] = jnp.zeros_like(carry2_ref)

    x = x_ref[...]
    h = _rms_mod(x, g_ref[...], mod_ref[1:2, :], mod_ref[0:1, :]).astype(BF16)
    for c in range(D // cw):
        lo = c * cw
        a = _dot(h, win_ref[:, lo:lo + cw])
        gate = _dot(h, win_ref[:, D + lo:D + lo + cw])
        glu = a * jax.nn.sigmoid(gate)
        ext = jnp.concatenate([carry_ref[:, lo:lo + cw], glu], axis=0)
        carry_ref[:, lo:lo + cw] = glu[tm - CF_HALO:, :]
        conv = jnp.zeros((tm, cw), F32) + bdw_ref[:, lo:lo + cw]
        for b in range(SUBLANES):
            r = pltpu.roll(ext, b, axis=0) if b else ext
            for a8 in range(CF_HALO // SUBLANES):
                s = a8 * SUBLANES + b
                if s < CONFORMER_K:
                    k = CONFORMER_K - 1 - s
                    start = CF_HALO - a8 * SUBLANES
                    conv = conv + wdw_ref[k:k + 1, lo:lo + cw] * r[start:start + tm, :]
        u_ref[:, lo:lo + cw] = conv
    u = u_ref[...]
    uc = u - jnp.mean(u, axis=-1, keepdims=True)
    y = uc * lax.rsqrt(jnp.mean(uc * uc, axis=-1, keepdims=True) + LN_EPS)
    y = y * lng_ref[...] + lnb_ref[...]
    y = (y * jax.nn.sigmoid(y)).astype(BF16)
    x = x + mod_ref[2:3, :] * _dot(y, wout_ref[...])
    o_ref[...] = _ffn_body(x, mod_ref, g2_ref, wup_ref, wc2_ref, wdn_ref, carry2_ref, act2_ref, tm, cw)


def _cf_layer(x, mod, g, w_in, w_dw, b_dw, ln_g, ln_b, w_out, ffn, layer, *, tm=512, cw=256):
    s = x.shape[0]
    g2, w_up, w_conv2, w_down = ffn
    return pl.pallas_call(
        functools.partial(_cf_kernel, tm=tm, cw=cw),
        grid=(s // tm,),
        in_specs=[_rows(tm, D), _resident((6, D)), _resident((1, D)), _resident((D, 2 * D)),
                  _resident((CONFORMER_K, D)), _resident((1, D)), _resident((1, D)),
                  _resident((1, D)), _resident((D, D))] + _ffn_specs(layer),
        out_specs=_rows(tm, D),
        out_shape=jax.ShapeDtypeStruct((s, D), F32),
        scratch_shapes=[pltpu.VMEM((CF_HALO, D), F32), pltpu.VMEM((tm, D), F32)] + _ffn_scratch(tm),
        compiler_params=_params("arbitrary"),
        name="cf_layer",
    )(x, mod, g.reshape(1, D), w_in, w_dw, b_dw.reshape(1, D), ln_g.reshape(1, D),
      ln_b.reshape(1, D), w_out, g2.reshape(1, D), w_up, w_conv2, w_down)


def _group_mean_matrix(n):
    r = lax.broadcasted_iota(jnp.int32, (n, n), 0) // HEAD_DIM
    c = lax.broadcasted_iota(jnp.int32, (n, n), 1) // HEAD_DIM
    return jnp.where(r == c, 1.0 / HEAD_DIM, 0.0).astype(BF16)


def _head_rms(z, gm):
    hi, lo = _split2(z * z)
    ms = _dot(hi, gm) + _dot(lo, gm)
    return z * lax.rsqrt(ms + RMS_EPS)


def _store_v_with_ones(v, v0_ref, v1_ref, lo, cw):
    vt = v.T
    feature = lax.broadcasted_iota(jnp.int32, vt.shape, 0)
    first = (feature & (LANES - 1)) < HEAD_DIM
    v0_ref[lo:lo + cw, :] = jnp.where(first, vt, 1.0).astype(BF16)
    v1_ref[lo:lo + cw, :] = jnp.where(first, 1.0, vt).astype(BF16)


def _fox_proj_kernel(x_ref, mod_ref, g_ref, w_ref, bf_ref, qn_ref, kn_ref,
                     q_ref, k_ref, v0_ref, v1_ref, kx_ref, f_ref, carry_ref, tri_ref, *, tm, cw):
    @pl.when(pl.program_id(0) == 0)
    def _():
        carry_ref[...] = jnp.zeros_like(carry_ref)
        r = lax.broadcasted_iota(jnp.int32, (tm, tm), 0)
        c = lax.broadcasted_iota(jnp.int32, (tm, tm), 1)
        tri_ref[...] = jnp.where(r >= c, 1.0, 0.0).astype(BF16)

    h = _rms_mod(x_ref[...], g_ref[...], mod_ref[1:2, :], mod_ref[0:1, :]).astype(BF16)
    gm = _group_mean_matrix(cw)
    for c in range(D // cw):
        lo = c * cw
        q = _head_rms(_dot(h, w_ref[:, lo:lo + cw]), gm)
        q_ref[:, lo:lo + cw] = (q * qn_ref[:, lo:lo + cw] * (SCALE * LOG2E)).astype(BF16)
        k = _head_rms(_dot(h, w_ref[:, D + lo:D + lo + cw]), gm)
        k_ref[:, lo:lo + cw] = (k * kn_ref[:, lo:lo + cw]).astype(BF16)
        _store_v_with_ones(_dot(h, w_ref[:, 2 * D + lo:2 * D + lo + cw]), v0_ref, v1_ref, lo, cw)
    fl = _dot(h, w_ref[:, 3 * D:3 * D + LANES]) + bf_ref[...]
    lf = jnp.minimum(fl, 0.0) - jnp.log1p(jnp.exp(-jnp.abs(fl)))
    tri = tri_ref[...]
    hi, mid, lo3 = _split3(lf)
    local = _dot(tri, hi) + _dot(tri, mid) + _dot(tri, lo3)
    cum = local + carry_ref[0:1, :]
    f_ref[...] = cum
    carry_ref[...] = jnp.broadcast_to(cum[tm - 1:tm, :], carry_ref.shape)
    head = lax.broadcasted_iota(jnp.int32, (LANES, D), 0)
    lane = lax.broadcasted_iota(jnp.int32, (LANES, D), 1)
    target = lax.shift_right_logical(head, 1) * LANES + (head & 1) * 3
    kx = jnp.zeros((tm, D), F32)
    for t, piece in enumerate(_split3(local * (-LOG2E))):
        place = jnp.where((lane == target + t) & (head < N_HEADS), 1.0, 0.0).astype(BF16)
        kx = kx + _dot(piece, place)
    kx_ref[...] = kx.astype(BF16)


def _fox_proj(x, mod, g, w_in, b_f, qn, kn, *, tm=ATTN_TILE, cw=256):
    s = x.shape[0]
    w = jnp.pad(w_in, ((0, 0), (0, LANES - N_HEADS))).astype(BF16)
    bf = jnp.pad(b_f, (0, LANES - N_HEADS)).reshape(1, LANES)
    return pl.pallas_call(
        functools.partial(_fox_proj_kernel, tm=tm, cw=cw),
        grid=(s // tm,),
        in_specs=[_rows(tm, D), _resident((6, D)), _resident((1, D)),
                  _resident((D, 3 * D + LANES)), _resident((1, LANES)),
                  _resident((1, D)), _resident((1, D))],
        out_specs=[_rows(tm, D), _rows(tm, D), _cols(tm), _cols(tm), _rows(tm, D), _rows(tm, LANES)],
        out_shape=[jax.ShapeDtypeStruct((s, D), BF16)] * 2 + [jax.ShapeDtypeStruct((D, s), BF16)] * 2
        + [jax.ShapeDtypeStruct((s, D), BF16), jax.ShapeDtypeStruct((s, LANES), F32)],
        scratch_shapes=[pltpu.VMEM((SUBLANES, LANES), F32), pltpu.VMEM((tm, tm), BF16)],
        compiler_params=_params("arbitrary"),
        name="fox_proj",
    )(x, mod, g.reshape(1, D), w, bf, jnp.tile(qn, N_HEADS).reshape(1, D),
      jnp.tile(kn, N_HEADS).reshape(1, D))


def _moba_proj_kernel(x_ref, mod_ref, g_ref, w_ref, qn_ref, kn_ref,
                      q_ref, k_ref, v0_ref, v1_ref, qf_ref, km_ref, kx_ref, *, tm, cw):
    row = lax.broadcasted_iota(jnp.int32, (tm, LANES), 0)
    lane = lax.broadcasted_iota(jnp.int32, (tm, LANES), 1)
    rem = row & (MOBA_BLOCK - 1)
    block = pl.program_id(0) * (tm // MOBA_BLOCK) + lax.shift_right_logical(row, MOBA_BLOCK.bit_length() - 1)
    offset_lanes = jnp.where(lane < MAX_KEY_BLOCKS + 3, rem, row - rem).astype(F32)
    kx_ref[...] = jnp.where(
        lane < MAX_KEY_BLOCKS, jnp.where(lane == block, 1.0, 0.0),
        jnp.where(lane < MAX_KEY_BLOCKS + 6, offset_lanes, 0.0)).astype(BF16)

    h = _rms_mod(x_ref[...], g_ref[...], mod_ref[1:2, :], mod_ref[0:1, :]).astype(BF16)
    gm = _group_mean_matrix(cw)
    for c in range(D // cw):
        lo = c * cw
        q = _head_rms(_dot(h, w_ref[:, lo:lo + cw]), gm) * qn_ref[:, lo:lo + cw]
        qf_ref[:, lo:lo + cw] = q
        q_ref[:, lo:lo + cw] = (q * (SCALE * LOG2E)).astype(BF16)
        k = _head_rms(_dot(h, w_ref[:, D + lo:D + lo + cw]), gm) * kn_ref[:, lo:lo + cw]
        k_ref[:, lo:lo + cw] = k.astype(BF16)
        for b in range(tm // MOBA_BLOCK):
            km_ref[b, :, lo:lo + cw] = jnp.mean(
                k[b * MOBA_BLOCK:(b + 1) * MOBA_BLOCK, :], axis=0, keepdims=True)
        _store_v_with_ones(_dot(h, w_ref[:, 2 * D + lo:2 * D + lo + cw]), v0_ref, v1_ref, lo, cw)


def _moba_proj(x, mod, g, w_in, qn, kn, *, tm=ATTN_TILE, cw=256):
    s = x.shape[0]
    nb = tm // MOBA_BLOCK
    return pl.pallas_call(
        functools.partial(_moba_proj_kernel, tm=tm, cw=cw),
        grid=(s // tm,),
        in_specs=[_rows(tm, D), _resident((6, D)), _resident((1, D)), _resident((D, 3 * D)),
                  _resident((1, D)), _resident((1, D))],
        out_specs=[_rows(tm, D), _rows(tm, D), _cols(tm), _cols(tm), _rows(tm, D),
                   pl.BlockSpec((nb, 1, D), lambda i: (i, 0, 0)), _rows(tm, LANES)],
        out_shape=[jax.ShapeDtypeStruct((s, D), BF16)] * 2 + [jax.ShapeDtypeStruct((D, s), BF16)] * 2
        + [jax.ShapeDtypeStruct((s, D), F32), jax.ShapeDtypeStruct((s // MOBA_BLOCK, 1, D), F32),
           jax.ShapeDtypeStruct((s, LANES), BF16)],
        compiler_params=_params("parallel"),
        name="moba_proj",
    )(x, mod, g.reshape(1, D), w_in, jnp.tile(qn, N_HEADS).reshape(1, D),
      jnp.tile(kn, N_HEADS).reshape(1, D))


def _moba_gate_kernel(q_ref, km_ref, slope_ref, qx_ref):
    own = pl.program_id(0)
    t = q_ref.shape[0]
    blk = lax.broadcasted_iota(jnp.int32, (MAX_KEY_BLOCKS, t), 0)
    lane = lax.broadcasted_iota(jnp.int32, (t, LANES), 1)
    km_lane_head = lax.shift_right_logical(
        lax.broadcasted_iota(jnp.int32, (MAX_KEY_BLOCKS, MXU_W), 1), HEAD_DIM.bit_length() - 1)
    pad_rows = jnp.full((LANES - MAX_KEY_BLOCKS, t), NEG_INF, F32)
    for g4 in range(D // MXU_W):
        lo = g4 * MXU_W
        q_hi, q_lo = _split2(q_ref[:, lo:lo + MXU_W])
        km = km_ref[:, lo:lo + MXU_W]
        for hl in range(MXU_W // HEAD_DIM):
            head = 4 * g4 + hl
            km_hi, km_lo = _split2(jnp.where(km_lane_head == hl, km, 0.0))
            gate = _dot_nt(km_hi, q_hi) + _dot_nt(km_lo, q_hi) + _dot_nt(km_hi, q_lo)
            keep = blk == own
            gh = jnp.where(blk < own, gate, NEG_INF)
            for _ in range(MOBA_TOPK):
                mx = jnp.max(gh, axis=0, keepdims=True)
                idx = jnp.min(jnp.where(gh == mx, blk, MAX_KEY_BLOCKS), axis=0, keepdims=True)
                pick = blk == idx
                keep = keep | (pick & (idx < own))
                gh = jnp.where(pick, -jnp.inf, gh)
            bias = jnp.concatenate([jnp.where(keep, 0.0, NEG_INF), pad_rows], axis=0).T
            qx = jnp.where(lane < MAX_KEY_BLOCKS, bias, slope_ref[head:head + 1, :])
            qx_ref[:, head * LANES:(head + 1) * LANES] = qx.astype(BF16)


def _moba_gate(qf, kmean, slope_lanes):
    s = qf.shape[0]
    n_kb = s // MOBA_BLOCK
    assert n_kb <= MAX_KEY_BLOCKS
    km = jnp.pad(kmean.reshape(n_kb, D), ((0, MAX_KEY_BLOCKS - n_kb), (0, 0)))
    return pl.pallas_call(
        _moba_gate_kernel,
        grid=(n_kb,),
        in_specs=[_rows(MOBA_BLOCK, D), _resident((MAX_KEY_BLOCKS, D)), _resident((N_HEADS, LANES))],
        out_specs=_rows(MOBA_BLOCK, N_HEADS * LANES),
        out_shape=jax.ShapeDtypeStruct((s, N_HEADS * LANES), BF16),
        compiler_params=_params("parallel"),
        name="moba_gate",
    )(qf, km, slope_lanes)


def _attn_kernel(*refs, t, moba):
    if moba:
        nvis_ref, r_ref, q_ref, kk_ref, kx_ref, v0_ref, v1_ref, qx_ref, o_ref, acc_ref, m_ref, s_ref, mp_ref = refs
    else:
        nvis_ref, r_ref, q_ref, kk_ref, kx_ref, v0_ref, v1_ref, o_ref, acc_ref, m_ref, s_ref, mp_ref = refs
    pair = pl.program_id(0)
    i = pl.program_id(1)
    q = q_ref[...]
    lane = lax.broadcasted_iota(jnp.int32, (t, LANES), 1)
    row = lax.broadcasted_iota(jnp.int32, (t, t), 0)
    col = lax.broadcasted_iota(jnp.int32, (t, t), 1)
    v_refs = (v0_ref, v1_ref)
    n_past = [nvis_ref[2 * pair + hh, i] for hh in range(2)]

    def query(hh):
        mine = (lane < HEAD_DIM) if hh == 0 else (lane >= HEAD_DIM)
        if moba:
            qx = qx_ref[:, hh * LANES:(hh + 1) * LANES]
        else:
            qx = jnp.where((lane >= 3 * hh) & (lane < 3 * hh + 3), 1.0, 0.0).astype(BF16)
        return jnp.concatenate([jnp.where(mine, q, jnp.zeros_like(q)), qx], axis=1)

    qh = [query(hh) for hh in range(2)]

    def scores(hh, j):
        start = pl.multiple_of(j * t, t)
        kk = jnp.concatenate([kk_ref[pl.ds(start, t), :], kx_ref[pl.ds(start, t), :]], axis=1)
        return _dot_nt(kk, qh[hh])

    def values(hh, j):
        return v_refs[hh][:, pl.ds(pl.multiple_of(j * t, t), t)]

    def down(stat, rows):
        return jnp.concatenate([stat] * (rows // SUBLANES), axis=0)

    def produce(hh, j, slot, causal=False):
        s = scores(hh, j)
        if causal:
            s = jnp.where(row <= col, s, NEG_INF)
        s_ref[hh, slot] = s
        part = s[0:SUBLANES, :]
        for g in range(1, t // SUBLANES):
            part = jnp.maximum(part, s[g * SUBLANES:(g + 1) * SUBLANES, :])
        mp_ref[hh, slot] = part

    def step(hh, j, cur, nxt=None):
        h = 2 * pair + hh
        if nxt is not None:
            produce(hh, jnp.maximum(j - 1, 0), nxt)
        s = s_ref[hh, cur]
        if moba:
            dist = jnp.zeros((1, 1), jnp.int32) + (i - j) * t
            c = -(r_ref[h, 0] * dist.astype(F32))
        else:
            c = (jnp.zeros((1, 1), F32) + r_ref[h, i]) - (jnp.zeros((1, 1), F32) + r_ref[h, j])
        m_old = m_ref[hh]
        m_tile = jnp.broadcast_to(jnp.max(mp_ref[hh, cur], axis=0, keepdims=True), (SUBLANES, t))
        m_new = jnp.maximum(m_old, m_tile + c)
        alpha = jnp.exp2(m_old - m_new)
        p = jnp.exp2(s - down(m_new - c, t))
        acc_ref[hh] = down(alpha, LANES) * acc_ref[hh] + _dot(values(hh, j), p.astype(BF16))
        m_ref[hh] = m_new

    for hh in range(2):
        acc_ref[hh] = jnp.zeros((LANES, t), F32)
        m_ref[hh] = jnp.full((SUBLANES, t), NEG_INF, F32)
        produce(hh, i, 0, causal=True)
    n_tiles = [n_past[hh] + 1 for hh in range(2)]

    n_joint = jnp.minimum(n_tiles[0], n_tiles[1]) // 2

    def joint_body(k2, carry):
        j = i - 2 * k2
        for hh in range(2):
            step(hh, j, 0, 1)
        for hh in range(2):
            step(hh, j - 1, 1, 0)
        return carry

    lax.fori_loop(0, n_joint, joint_body, 0)

    outs = []
    for hh in range(2):
        def tail_body(k2, carry, hh=hh):
            j = i - 2 * k2
            step(hh, j, 0, 1)
            step(hh, j - 1, 1, 0)
            return carry

        n_pairs = n_tiles[hh] // 2
        lax.fori_loop(n_joint, n_pairs, tail_body, 0)

        @pl.when(n_tiles[hh] % 2 == 1)
        def _(hh=hh, n_pairs=n_pairs):
            step(hh, i - 2 * n_pairs, 0)

        acc = acc_ref[hh]
        own = acc[hh * HEAD_DIM:(hh + 1) * HEAD_DIM, :]
        sums = acc[(1 - hh) * HEAD_DIM:(2 - hh) * HEAD_DIM, :]
        outs.append(own * (1.0 / sums))
    o_ref[...] = jnp.concatenate(outs, axis=0).T.astype(BF16)


def _attention(nvis, r, q, kk, kx, v0, v1, qx=None, *, t):
    s = q.shape[0]
    moba = qx is not None
    col_block = pl.BlockSpec((s, LANES), lambda p, i, *_: (0, p))
    row_block = pl.BlockSpec((LANES, s), lambda p, i, *_: (p, 0))
    tile = pl.BlockSpec((t, LANES), lambda p, i, *_: (i, p))
    ins = [q, kk, kx, v0, v1]
    if moba:
        kx_spec = pl.BlockSpec((s, LANES), lambda p, i, *_: (0, 0), pipeline_mode=pl.Buffered(1))
        specs = [tile, col_block, kx_spec, row_block, row_block,
                 pl.BlockSpec((t, 2 * LANES), lambda p, i, *_: (i, p))]
        ins.append(qx)
    else:
        specs = [tile, col_block, col_block, row_block, row_block]
    return pl.pallas_call(
        functools.partial(_attn_kernel, t=t, moba=moba),
        grid_spec=pltpu.PrefetchScalarGridSpec(
            num_scalar_prefetch=2,
            grid=(N_PAIRS, s // t),
            in_specs=specs,
            out_specs=tile,
            scratch_shapes=[pltpu.VMEM((2, LANES, t), F32), pltpu.VMEM((2, SUBLANES, t), F32),
                            pltpu.VMEM((2, 2, t, t), F32), pltpu.VMEM((2, 2, SUBLANES, t), F32)],
        ),
        out_shape=jax.ShapeDtypeStruct((s, D), BF16),
        compiler_params=_params("parallel", "parallel"),
        name="moba_attn" if moba else "fox_attn",
    )(nvis, r, *ins)


def _first_visible(vis):
    nt = vis.shape[1]
    i = jnp.arange(nt)[None, :, None]
    j = jnp.arange(nt)[None, None, :]
    first = jnp.min(jnp.where(vis & (j < i), j, i), axis=2)
    return (i[:, :, 0] - first).astype(jnp.int32)


def _score_bound(qn, kn):
    return 1.02 * HEAD_DIM ** 0.5 * jnp.max(jnp.abs(qn)) * jnp.max(jnp.abs(kn))


def _fox_schedule(cum_f, qn, kn, t):
    f = cum_f[:, :N_HEADS].T
    f_before = jnp.concatenate([jnp.zeros((N_HEADS, 1), F32), f[:, t - 1:-1:t]], axis=1)
    reach = SKIP_THRESHOLD + 2.0 * _score_bound(qn, kn)
    vis = (f[:, 0::t, None] - f[:, None, t - 1::t]) >= -reach
    return _first_visible(vis), LOG2E * f_before


def _moba_constants(s, qn, kn, t):
    slopes = jnp.exp2(-ALIBI_MAX_EXP * jnp.arange(1, N_HEADS + 1, dtype=F32) / N_HEADS)
    hi, mid, lo = _split3(slopes * LOG2E)
    pieces = jnp.stack([hi, mid, lo, hi, mid, lo], axis=1).astype(F32)
    slope_lanes = jnp.zeros((N_HEADS, LANES), F32).at[:, MAX_KEY_BLOCKS:MAX_KEY_BLOCKS + 6].set(pieces)
    nt = s // t
    i = jnp.arange(nt)[None, :, None]
    j = jnp.arange(nt)[None, None, :]
    nearest = ((i - j - 1) * t + 1).astype(F32)
    reach = SKIP_THRESHOLD + 2.0 * _score_bound(qn, kn)
    nvis = _first_visible(slopes[:, None, None] * nearest <= reach)
    r = jnp.broadcast_to((slopes * LOG2E)[:, None], (N_HEADS, nt))
    return slope_lanes, nvis, r


def kernel(x, c, ln_mix_g, ln_ffn_g, ada_w, ada_b, ffn_up, ffn_conv, ffn_down, sc_in, sc_conv, sc_out, cf_in, cf_dw, cf_dw_b, cf_ln_g, cf_ln_b, cf_out, fox_in, fox_bf, fox_qn, fox_kn, fox_out, moba_in, moba_qn, moba_kn, moba_out):
    batch, s, _ = x.shape
    assert batch == 1 and c.shape[0] == 1
    depth = ada_w.shape[0]
    mods = _ada(c, ada_w, ada_b)
    ffn_up_bf16 = ffn_up.astype(BF16)
    ffn_down_bf16 = ffn_down.astype(BF16)
    xs = x.reshape(s, D)
    for i in range(depth):
        kind, j = i % 4, i // 4
        mod = mods[i]
        ffn = (ln_ffn_g[i], ffn_up_bf16, ffn_conv[i], ffn_down_bf16)
        if kind == 0:
            xs = _sc_layer(xs, mod, ln_mix_g[i], sc_in[j].astype(BF16), sc_conv[j],
                           sc_out[j].astype(BF16), ffn, i)
            continue
        if kind == 1:
            xs = _cf_layer(xs, mod, ln_mix_g[i], cf_in[j].astype(BF16), cf_dw[j], cf_dw_b[j],
                           cf_ln_g[j], cf_ln_b[j], cf_out[j].astype(BF16), ffn, i)
            continue
        if kind == 2:
            q, k, v0, v1, kx, cum_f = _fox_proj(xs, mod, ln_mix_g[i], fox_in[j], fox_bf[j],
                                                fox_qn[j], fox_kn[j])
            nvis, r = _fox_schedule(cum_f, fox_qn[j], fox_kn[j], ATTN_TILE)
            pre = (_attention(nvis, r, q, k, kx, v0, v1, t=ATTN_TILE), fox_out[j].astype(BF16))
        else:
            q, k, v0, v1, qf, kmean, kx = _moba_proj(xs, mod, ln_mix_g[i], moba_in[j].astype(BF16),
                                                     moba_qn[j], moba_kn[j])
            slope_lanes, nvis, r = _moba_constants(s, moba_qn[j], moba_kn[j], ATTN_TILE)
            qx = _moba_gate(qf, kmean, slope_lanes)
            pre = (_attention(nvis, r, q, k, kx, v0, v1, qx, t=ATTN_TILE), moba_out[j].astype(BF16))
        xs = _ffn(xs, mod, ln_ffn_g[i], ffn_up_bf16, ffn_conv[i], ffn_down_bf16, i, pre)
    return xs.reshape(1, s, D)
```
